```python
import math
import jax, jax.numpy as jnp
from jax import lax
import numpy as np

D_MODEL = 1024
BATCH = 4
SEQ = 8192
DEPTH = 1
DEC_BATCH = 4
DEC_SEQ = 4096
PAST_LEN = 128

GDN_HEADS = 8
GDN_DK = 128
GDN_DV = 128
GDN_CONV = 3
GDN_CHUNK = 64
SC_WIDTH = 1024
SC_CONV = 3
PEER_HEADS = 8
PEER_NKEYS = 128
PEER_EXPERTS = PEER_NKEYS * PEER_NKEYS
PEER_DQ_HALF = 128
PEER_TOPK = 16
PEER_TOKEN_BLOCK = 128
PEER_Q_W = PEER_HEADS * 2 * PEER_DQ_HALF
NORM_EPS = 1e-6
N_MOD = 6

QKV_W = 2 * GDN_HEADS * GDN_DK + GDN_HEADS * GDN_DV
Z_W = GDN_HEADS * GDN_DV
BETA_W = 2 * GDN_HEADS
ALPHA_W = 2 * GDN_HEADS
SC_IN_W = 3 * SC_WIDTH
MERGE_W = 2 * D_MODEL
IN_W = QKV_W + Z_W + BETA_W + ALPHA_W + SC_IN_W + MERGE_W
IN_OFFSETS = (QKV_W, QKV_W + Z_W, QKV_W + Z_W + BETA_W, QKV_W + Z_W + BETA_W + ALPHA_W,
              QKV_W + Z_W + BETA_W + ALPHA_W + SC_IN_W)

kernel_name = 'hybrid_gdn_shortconv_peer_encoder'


def rms_norm(x, g):
    x32 = x.astype(jnp.float32)
    y = x32 * lax.rsqrt(jnp.mean(x32 * x32, axis=-1, keepdims=True) + NORM_EPS)
    return (y * g.astype(jnp.float32)).astype(x.dtype)


def l2_normalize(x):
    x32 = x.astype(jnp.float32)
    return x32 * lax.rsqrt(jnp.sum(x32 * x32, axis=-1, keepdims=True) + NORM_EPS)


def centred_depthwise_conv(x, w):
    k, c = w.shape
    pad = (k - 1) // 2
    return lax.conv_general_dilated(
        x, w[:, None, :].astype(x.dtype), window_strides=(1,), padding=[(pad, pad)],
        dimension_numbers=('NWC', 'WIO', 'NWC'), feature_group_count=c)


def chunk_gated_delta_rule(q, k, v, g, beta):
    n, s, h, dk = q.shape
    dv = v.shape[-1]
    c = GDN_CHUNK
    nc = s // c
    f32 = jnp.float32
    q = q.astype(f32) * (dk ** -0.5)
    k = k.astype(f32)
    v = v.astype(f32)
    to_chunks = lambda t: t.reshape(n, nc, c, h, t.shape[-1]).transpose(0, 3, 1, 2, 4)
    q, k, v = to_chunks(q), to_chunks(k), to_chunks(v)
    g = jnp.cumsum(g.astype(f32).reshape(n, nc, c, h).transpose(0, 3, 1, 2), axis=-1)
    beta = beta.astype(f32).reshape(n, nc, c, h).transpose(0, 3, 1, 2)
    idx = jnp.arange(c)
    incl = idx[:, None] >= idx[None, :]
    strict = idx[:, None] > idx[None, :]
    decay = jnp.exp(jnp.where(incl, g[..., :, None] - g[..., None, :], -jnp.inf))
    kb = k * beta[..., None]
    lower = jnp.where(strict, jnp.einsum('nhcid,nhcjd->nhcij', kb, k) * decay, 0.0)
    rhs = jnp.concatenate([v * beta[..., None], kb * jnp.exp(g)[..., None]], axis=-1)
    sol = lax.linalg.triangular_solve(jnp.eye(c, dtype=f32) + lower, rhs,
                                      left_side=True, lower=True, unit_diagonal=True)
    u, w = sol[..., :dv], sol[..., dv:]
    attn = jnp.where(incl, jnp.einsum('nhcid,nhcjd->nhcij', q, k) * decay, 0.0)
    xs = tuple(jnp.moveaxis(t, 2, 0) for t in (q, k, u, w, g, attn))

    def step(state, inp):
        qc, kc, uc, wc, gc, ac = inp
        v_new = uc - jnp.einsum('nhck,nhkv->nhcv', wc, state)
        o = (jnp.einsum('nhck,nhkv->nhcv', qc * jnp.exp(gc)[..., None], state)
             + jnp.einsum('nhij,nhjv->nhiv', ac, v_new))
        g_last = gc[..., -1:]
        state = (state * jnp.exp(g_last)[..., None]
                 + jnp.einsum('nhck,nhcv->nhkv', kc * jnp.exp(g_last - gc)[..., None], v_new))
        return state, o

    state0 = jnp.zeros((n, h, dk, dv), f32)
    _, os = lax.scan(step, state0, xs)
    return os.transpose(1, 0, 3, 2, 4).reshape(n, s, h, dv)


def gated_deltanet_bidir(q, k, v, alpha, beta_in, a_log, dt_bias):
    bsz = q.shape[0]
    f32 = jnp.float32
    g = -jnp.exp(a_log.astype(f32)) * jax.nn.softplus(alpha.astype(f32) + dt_bias.astype(f32))
    beta = jax.nn.sigmoid(beta_in.astype(f32))
    flip = lambda t: jnp.flip(t, axis=1)
    qq = jnp.concatenate([q, flip(q)], axis=0)
    kk = jnp.concatenate([k, flip(k)], axis=0)
    vv = jnp.concatenate([v, flip(v)], axis=0)
    gg = jnp.concatenate([g[:, :, 0], flip(g[:, :, 1])], axis=0)
    bb = jnp.concatenate([beta[:, :, 0], flip(beta[:, :, 1])], axis=0)
    o = chunk_gated_delta_rule(qq, kk, vv, gg, bb)
    return o[:bsz] + flip(o[bsz:])


def peer(h, w_pq, sub_keys, u_tab, v_tab):
    t = h.shape[0]
    qry = (h @ w_pq).reshape(t, PEER_HEADS, 2, PEER_DQ_HALF)
    s = jnp.einsum('thpd,hpkd->thpk', qry, sub_keys).astype(jnp.float32)
    sv, si = lax.top_k(s, PEER_TOPK)
    cand = (sv[..., 0, :, None] + sv[..., 1, None, :]).reshape(t, PEER_HEADS, PEER_TOPK * PEER_TOPK)
    cidx = (si[..., 0, :, None] * PEER_NKEYS + si[..., 1, None, :]).reshape(t, PEER_HEADS, PEER_TOPK * PEER_TOPK)
    tv, tp = lax.top_k(cand, PEER_TOPK)
    eidx = jnp.take_along_axis(cidx, tp, axis=-1)
    gates = jax.nn.softmax(tv, axis=-1).astype(h.dtype)
    nb = t // PEER_TOKEN_BLOCK

    def block(args):
        hb, ib, gb = args
        act = jax.nn.gelu(jnp.einsum('thkd,td->thk', u_tab[ib], hb), approximate=False)
        return jnp.einsum('thk,thkd->td', gb * act, v_tab[ib])

    out = lax.map(block, (h.reshape(nb, PEER_TOKEN_BLOCK, -1),
                          eidx.reshape(nb, PEER_TOKEN_BLOCK, PEER_HEADS, PEER_TOPK),
                          gates.reshape(nb, PEER_TOKEN_BLOCK, PEER_HEADS, PEER_TOPK)))
    return out.reshape(t, -1)


def encoder_layer(x, c, pre1, post1, pre2, post2, w_ada, b_ada, w_in, conv_qkv, a_log, dt_bias,
                  out_norm, w_a, conv_sc, w_b, w_o, w_pq, sub_keys, u_tab, v_tab):
    bsz, seq, d = x.shape
    mod = jax.nn.silu(c) @ w_ada + b_ada
    sh1, sc1, gt1, sh2, sc2, gt2 = [m[:, None, :] for m in jnp.split(mod, N_MOD, axis=-1)]

    h = rms_norm(x, pre1) * (1 + sc1) + sh1
    zin = h @ w_in
    qkv, z, bta, alf, sc_in, mg = jnp.split(zin, IN_OFFSETS, axis=-1)
    qkv = jax.nn.silu(centred_depthwise_conv(qkv, conv_qkv))
    q, k, v = jnp.split(qkv, [GDN_HEADS * GDN_DK, 2 * GDN_HEADS * GDN_DK], axis=-1)
    q = l2_normalize(q.reshape(bsz, seq, GDN_HEADS, GDN_DK))
    k = l2_normalize(k.reshape(bsz, seq, GDN_HEADS, GDN_DK))
    v = v.reshape(bsz, seq, GDN_HEADS, GDN_DV)
    o = gated_deltanet_bidir(q, k, v, alf.reshape(bsz, seq, 2, GDN_HEADS),
                             bta.reshape(bsz, seq, 2, GDN_HEADS), a_log, dt_bias)
    o = rms_norm(o, out_norm) * jax.nn.silu(z.reshape(bsz, seq, GDN_HEADS, GDN_DV).astype(jnp.float32))
    y_a = o.reshape(bsz, seq, GDN_HEADS * GDN_DV).astype(x.dtype) @ w_a

    sx, sb, scg = jnp.split(sc_in, 3, axis=-1)
    y_b = (sb * centred_depthwise_conv(scg * sx, conv_sc)) @ w_b

    ga, gb = jnp.split(jax.nn.sigmoid(mg), 2, axis=-1)
    mix = (ga * y_a + gb * y_b) @ w_o
    x = x + gt1 * rms_norm(mix, post1)

    h2 = rms_norm(x, pre2) * (1 + sc2) + sh2
    f = peer(h2.reshape(bsz * seq, d), w_pq, sub_keys, u_tab, v_tab).reshape(bsz, seq, d)
    x = x + gt2 * rms_norm(f, post2)
    return x


def setup_inputs(seed: int = 0) -> dict:
    key = jax.random.key(seed)
    ks = jax.random.split(key, 24)
    f32 = jnp.float32
    D = D_MODEL
    nrm = lambda k, shape, scale: jax.random.normal(k, shape, f32) * scale
    dt = jnp.exp(jax.random.uniform(ks[12], (DEPTH, 2, GDN_HEADS), f32, math.log(1e-3), math.log(1e-1)))
    return {
        'x_prompt': nrm(ks[0], (BATCH, SEQ, D), 1.0),
        'x_sample': nrm(ks[1], (DEC_BATCH, DEC_SEQ, D), 1.0),
        'c_prompt': nrm(ks[2], (BATCH, D), 1.0),
        'c_sample': nrm(ks[3], (DEC_BATCH, D), 1.0),
        'pre_norm1': 1.0 + nrm(ks[4], (DEPTH, D), 0.05),
        'post_norm1': 1.0 + nrm(ks[5], (DEPTH, D), 0.05),
        'pre_norm2': 1.0 + nrm(ks[6], (DEPTH, D), 0.05),
        'post_norm2': 1.0 + nrm(ks[7], (DEPTH, D), 0.05),
        'w_ada': nrm(ks[8], (DEPTH, D, N_MOD * D), 0.5 * D ** -0.5),
        'b_ada': nrm(ks[9], (DEPTH, N_MOD * D), 0.02),
        'w_in': nrm(ks[10], (DEPTH, D, IN_W), D ** -0.5),
        'conv_qkv': nrm(ks[11], (DEPTH, GDN_CONV, QKV_W), GDN_CONV ** -0.5),
        'a_log': jnp.log(jax.random.uniform(ks[13], (DEPTH, 2, GDN_HEADS), f32, 1.0, 16.0)),
        'dt_bias': dt + jnp.log(-jnp.expm1(-dt)),
        'out_norm': 1.0 + nrm(ks[14], (DEPTH, GDN_DV), 0.05),
        'w_branch_a': nrm(ks[15], (DEPTH, GDN_HEADS * GDN_DV, D), (GDN_HEADS * GDN_DV) ** -0.5),
        'conv_sc': nrm(ks[16], (DEPTH, SC_CONV, SC_WIDTH), SC_CONV ** -0.5),
        'w_branch_b': nrm(ks[17], (DEPTH, SC_WIDTH, D), SC_WIDTH ** -0.5),
        'w_out': nrm(ks[18], (DEPTH, D, D), D ** -0.5),
        'w_peer_q': nrm(ks[19], (DEPTH, D, PEER_Q_W), D ** -0.5),
        'peer_sub_keys': nrm(ks[20], (DEPTH, PEER_HEADS, 2, PEER_NKEYS, PEER_DQ_HALF), PEER_DQ_HALF ** -0.5),
        'peer_u': nrm(ks[21], (DEPTH, PEER_EXPERTS, D), D ** -0.5),
        'peer_v': nrm(ks[22], (DEPTH, PEER_EXPERTS, D), D ** -0.5),
    }


def reference(x_prompt, x_sample, c_prompt, c_sample, pre_norm1, post_norm1, pre_norm2, post_norm2,
              w_ada, b_ada, w_in, conv_qkv, a_log, dt_bias, out_norm, w_branch_a, conv_sc, w_branch_b,
              w_out, w_peer_q, peer_sub_keys, peer_u, peer_v):
    def run(x, c):
        for l in range(DEPTH):
            x = encoder_layer(x, c, pre_norm1[l], post_norm1[l], pre_norm2[l], post_norm2[l],
                              w_ada[l], b_ada[l], w_in[l], conv_qkv[l], a_log[l], dt_bias[l],
                              out_norm[l], w_branch_a[l], conv_sc[l], w_branch_b[l], w_out[l],
                              w_peer_q[l], peer_sub_keys[l], peer_u[l], peer_v[l])
        return x

    y_prompt = run(x_prompt, c_prompt)
    y_sample = run(x_sample, c_sample)
    return (y_prompt, y_sample)
```

```python
import functools
import math

import jax
import jax.numpy as jnp
from jax import lax
from jax.experimental import pallas as pl
from jax.experimental.pallas import tpu as pltpu

F32 = jnp.float32
BF16 = jnp.bfloat16

NORM_EPS = 1e-6
N_MOD = 6
MOD_ROWS = 8
GDN_HEADS = 8
GDN_DK = 128
GDN_DV = 128
GDN_CHUNK = 64
PEER_HEADS = 8
PEER_NKEYS = 128
PEER_TOPK = 16
LANES = 128
HALO = 16
AB_W = 128
AUX_ROWS = 24
NEG_BIG = -1e30
VMEM_LIMIT = 56 * 1024 * 1024


def _silu(x):
    return x * jax.nn.sigmoid(x)


def _softplus(x):
    return jnp.maximum(x, 0.0) + jnp.log1p(jnp.exp(-jnp.abs(x)))


def _rms(x, g):
    return x * lax.rsqrt(jnp.mean(x * x, axis=-1, keepdims=True) + NORM_EPS) * g


def _dot(a, b):
    return jnp.dot(a, b, preferred_element_type=F32)


def _cparams(*sem):
    return pltpu.CompilerParams(dimension_semantics=sem, vmem_limit_bytes=VMEM_LIMIT)


def _ada_kernel(c_ref, w_ref, b_ref, o_ref):
    c = c_ref[...]
    o_ref[...] = _dot(_silu(c).astype(BF16), w_ref[...].astype(BF16)) + b_ref[...]


def _ada(c, w, b, tn=1536):
    m, d = c.shape
    n = w.shape[1]
    return pl.pallas_call(
        _ada_kernel,
        grid=(n // tn,),
        in_specs=[pl.BlockSpec((m, d), lambda j: (0, 0)),
                  pl.BlockSpec((d, tn), lambda j: (0, j)),
                  pl.BlockSpec((1, tn), lambda j: (0, j))],
        out_specs=pl.BlockSpec((m, tn), lambda j: (0, j)),
        out_shape=jax.ShapeDtypeStruct((m, n), F32),
        compiler_params=_cparams("parallel"),
        name="ada",
    )(c, w, b.reshape(1, n))


def _inproj_kernel(x_ref, mod_ref, g_ref, w_ref, wab_ref, zin_ref, ab_ref, h_scr):
    @pl.when(pl.program_id(1) == 0)
    def _():
        y = _rms(x_ref[...], g_ref[...])
        h = y * (1.0 + mod_ref[0, 1:2, :]) + mod_ref[0, 0:1, :]
        hb = h.astype(BF16)
        h_scr[...] = hb
        ab_ref[...] = _dot(hb, wab_ref[...])

    zin_ref[...] = _dot(h_scr[...], w_ref[...]).astype(BF16)


def _inproj(x2, mod, g, w_main, w_ab, seq, tm, tn=1024):
    t, d = x2.shape
    n = w_main.shape[1]
    tps = seq // tm
    return pl.pallas_call(
        _inproj_kernel,
        grid=(t // tm, n // tn),
        in_specs=[pl.BlockSpec((tm, d), lambda i, j: (i, 0)),
                  pl.BlockSpec((1, MOD_ROWS, d), lambda i, j: (i // tps, 0, 0)),
                  pl.BlockSpec((1, d), lambda i, j: (0, 0)),
                  pl.BlockSpec((d, tn), lambda i, j: (0, j)),
                  pl.BlockSpec((d, AB_W), lambda i, j: (0, 0))],
        out_specs=[pl.BlockSpec((tm, tn), lambda i, j: (i, j)),
                   pl.BlockSpec((tm, AB_W), lambda i, j: (i, 0))],
        out_shape=[jax.ShapeDtypeStruct((t, n), BF16),
                   jax.ShapeDtypeStruct((t, AB_W), F32)],
        scratch_shapes=[pltpu.VMEM((tm, d), BF16)],
        compiler_params=_cparams("parallel", "arbitrary"),
        name="inproj",
    )(x2, mod, g, w_main, w_ab)


def _shift_rows(x, prev_row, next_row):
    n = x.shape[0]
    rows = lax.broadcasted_iota(jnp.int32, (n, 1), 0)
    xm = jnp.where(rows == 0, prev_row, pltpu.roll(x, 1, axis=0))
    xp = jnp.where(rows == n - 1, next_row, pltpu.roll(x, n - 1, axis=0))
    return xm, xp


def _conv_kernel(qkv_ref, qp_ref, qn_ref, sc_ref, sp_ref, sn_ref, ab_ref, cq_ref, cs_ref,
                 arow_ref, dtrow_ref, qkvn_ref, kt_ref, yb_ref, gs_ref, aux_ref, *, tiles_per_seq):
    i = pl.program_id(0)
    tb = qkv_ref.shape[0]
    nch = tb // GDN_CHUNK
    hw = GDN_HEADS * GDN_DK
    keep_prev = ((i % tiles_per_seq) != 0).astype(F32)
    keep_next = ((i % tiles_per_seq) != tiles_per_seq - 1).astype(F32)

    x = qkv_ref[...].astype(F32)
    xm, xp = _shift_rows(x, qp_ref[HALO - 1:HALO, :].astype(F32) * keep_prev,
                         qn_ref[0:1, :].astype(F32) * keep_next)
    y = _silu(cq_ref[0:1, :] * xm + cq_ref[1:2, :] * x + cq_ref[2:3, :] * xp)
    for hh in range(2 * GDN_HEADS):
        seg = y[:, hh * GDN_DK:(hh + 1) * GDN_DK]
        nrm = seg * lax.rsqrt(jnp.sum(seg * seg, axis=-1, keepdims=True) + NORM_EPS)
        if hh < GDN_HEADS:
            nrm = nrm * (GDN_DK ** -0.5)
        qkvn_ref[:, hh * GDN_DK:(hh + 1) * GDN_DK] = nrm.astype(BF16)
        if hh >= GDN_HEADS:
            kt = nrm.T
            for c in range(nch):
                kt_ref[c, (hh - GDN_HEADS) * GDN_DK:(hh - GDN_HEADS + 1) * GDN_DK, :] = (
                    kt[:, c * GDN_CHUNK:(c + 1) * GDN_CHUNK].astype(BF16))
    qkvn_ref[:, 2 * hw:] = y[:, 2 * hw:].astype(BF16)

    w = sc_ref.shape[1] // 3
    s = sc_ref[...].astype(F32)
    p = s[:, 2 * w:] * s[:, :w]
    sp = sp_ref[HALO - 1:HALO, :].astype(F32)
    sn = sn_ref[0:1, :].astype(F32)
    pm, pp = _shift_rows(p, sp[:, 2 * w:] * sp[:, :w] * keep_prev, sn[:, 2 * w:] * sn[:, :w] * keep_next)
    yb_ref[...] = (s[:, w:2 * w] * (cs_ref[0:1, :] * pm + cs_ref[1:2, :] * p + cs_ref[2:3, :] * pp)).astype(BF16)

    ab = ab_ref[...]
    g = -arow_ref[...] * _softplus(ab + dtrow_ref[...])
    rows = lax.broadcasted_iota(jnp.int32, (tb, 1), 0) % GDN_CHUNK
    lane = lax.broadcasted_iota(jnp.int32, (1, AB_W), 1)
    pre, suf = g, g
    sh = 1
    while sh < GDN_CHUNK:
        pre = pre + jnp.where(rows >= sh, pltpu.roll(pre, sh, axis=0), 0.0)
        suf = suf + jnp.where(rows < GDN_CHUNK - sh, pltpu.roll(suf, tb - sh, axis=0), 0.0)
        sh *= 2
    tot = pre + suf - g
    gc = jnp.where(lane >= AB_W // 2, suf, pre)
    grp = (lane % (AB_W // 2)) // GDN_HEADS
    gs = jnp.where(grp == 0, gc,
                   jnp.where(grp == 1, jax.nn.sigmoid(ab),
                             jnp.where(grp == 2, jnp.exp(gc),
                                       jnp.where(grp == 3, jnp.exp(tot - gc), jnp.exp(tot)))))
    gs_ref[...] = gs
    gst = gs.T
    for c in range(nch):
        sl = slice(c * GDN_CHUNK, (c + 1) * GDN_CHUNK)
        for d in range(2):
            base = d * (AB_W // 2)
            blk = jnp.zeros((AUX_ROWS, LANES), F32)
            aux_ref[c, d] = blk
            aux_ref[c, d, 0:8, 0:GDN_CHUNK] = gst[base:base + 8, sl]
            aux_ref[c, d, 8:16, 0:GDN_CHUNK] = gst[base + 24:base + 32, sl]
            aux_ref[c, d, 16:24, :] = jnp.broadcast_to(
                gst[base + 32:base + 40, c * GDN_CHUNK:c * GDN_CHUNK + 1], (8, LANES))


def _conv(zin, ab, conv_qkv, conv_sc, arow, dtrow, seq, tb):
    t = zin.shape[0]
    qw = conv_qkv.shape[1]
    sw = conv_sc.shape[1]
    hw = GDN_HEADS * GDN_DK
    assert qw == 3 * hw and zin.shape[1] % qw == 0
    nt = t // tb
    tps = seq // tb
    hb = tb // HALO
    nhalo = t // HALO
    prev = lambda c: (lambda i: (jnp.maximum(i * hb - 1, 0), c))
    nxt = lambda c: (lambda i: (jnp.minimum((i + 1) * hb, nhalo - 1), c))
    nch = tb // GDN_CHUNK
    return pl.pallas_call(
        functools.partial(_conv_kernel, tiles_per_seq=tps),
        grid=(nt,),
        in_specs=[pl.BlockSpec((tb, qw), lambda i: (i, 0)),
                  pl.BlockSpec((HALO, qw), prev(0)),
                  pl.BlockSpec((HALO, qw), nxt(0)),
                  pl.BlockSpec((tb, qw), lambda i: (i, 1)),
                  pl.BlockSpec((HALO, qw), prev(1)),
                  pl.BlockSpec((HALO, qw), nxt(1)),
                  pl.BlockSpec((tb, AB_W), lambda i: (i, 0)),
                  pl.BlockSpec((3, qw), lambda i: (0, 0)),
                  pl.BlockSpec((3, sw), lambda i: (0, 0)),
                  pl.BlockSpec((1, AB_W), lambda i: (0, 0)),
                  pl.BlockSpec((1, AB_W), lambda i: (0, 0))],
        out_specs=[pl.BlockSpec((tb, qw), lambda i: (i, 0)),
                   pl.BlockSpec((nch, hw, GDN_CHUNK), lambda i: (i, 0, 0)),
                   pl.BlockSpec((tb, sw), lambda i: (i, 0)),
                   pl.BlockSpec((tb, AB_W), lambda i: (i, 0)),
                   pl.BlockSpec((nch, 2, AUX_ROWS, LANES), lambda i: (i, 0, 0, 0))],
        out_shape=[jax.ShapeDtypeStruct((t, qw), BF16),
                   jax.ShapeDtypeStruct((t // GDN_CHUNK, hw, GDN_CHUNK), BF16),
                   jax.ShapeDtypeStruct((t, sw), BF16),
                   jax.ShapeDtypeStruct((t, AB_W), F32),
                   jax.ShapeDtypeStruct((t // GDN_CHUNK, 2, AUX_ROWS, LANES), F32)],
        compiler_params=_cparams("parallel"),
        name="conv",
    )(zin, zin, zin, zin, zin, zin, ab, conv_qkv, conv_sc, arow, dtrow)


def _gdn_kernel(q_ref, k_ref, v_ref, kt_ref, gs_ref, aux_ref, o_ref, s_scr):
    d = pl.program_id(1)

    @pl.when(pl.program_id(2) == 0)
    def _():
        s_scr[...] = jnp.zeros_like(s_scr)

    c = GDN_CHUNK
    sign = 1 - 2 * d
    ri = lax.broadcasted_iota(jnp.int32, (c, c), 0)
    ci = lax.broadcasted_iota(jnp.int32, (c, c), 1)
    diff = (ri - ci) * sign
    incl = diff >= 0
    strict = diff > 0
    gs = gs_ref[...]
    gsd = jnp.where(d == 1, pltpu.roll(gs, AB_W // 2, axis=1), gs)

    for h in range(GDN_HEADS):
        hs = slice(h * GDN_DK, (h + 1) * GDN_DK)
        q = q_ref[:, hs]
        k = k_ref[:, hs].astype(F32)
        v = v_ref[:, hs].astype(F32)
        kt = kt_ref[0, hs, :]
        gcol = gsd[:, h:h + 1]
        beta = gsd[:, 8 + h:9 + h]
        egc = gsd[:, 16 + h:17 + h]
        grow = aux_ref[0, 0, h:h + 1, 0:c]
        kdrow = aux_ref[0, 0, 8 + h:9 + h, 0:c]
        etot = aux_ref[0, 0, 16 + h:17 + h, :]

        decay = jnp.exp(jnp.where(incl, gcol - grow, NEG_BIG))
        kb = k * beta
        kbb = kb.astype(BF16)
        low = jnp.where(strict, _dot(kbb, kt) * decay, 0.0)
        attn = jnp.where(incl, _dot(q, kt) * decay, 0.0)

        x = jnp.concatenate([v * beta, kb * egc], axis=-1)
        lp = low.astype(BF16)
        x = x - _dot(lp, x.astype(BF16))
        p = 2
        while p < c:
            lpf = _dot(lp, lp)
            lp = lpf.astype(BF16)
            x = x + _dot(lp, x.astype(BF16))
            p *= 2
        u = x[:, :GDN_DV]
        w = x[:, GDN_DV:]

        s = s_scr[h]
        sb = s.astype(BF16)
        v_new = u - _dot(w.astype(BF16), sb)
        vnb = v_new.astype(BF16)
        qg = (q.astype(F32) * egc).astype(BF16)
        o = _dot(qg, sb) + _dot(attn.astype(BF16), vnb)
        o_ref[0, :, hs] = o.astype(BF16)
        ktd = (kt.astype(F32) * kdrow).astype(BF16)
        s_scr[h] = s * etot + _dot(ktd, vnb)


def _gdn(qkvn, kt, gs, aux, batch, seq):
    t = qkvn.shape[0]
    hw = GDN_HEADS * GDN_DK
    c = GDN_CHUNK
    nc = seq // c

    def chunk(b, d, j):
        return b * nc + j + d * (nc - 1 - 2 * j)

    return pl.pallas_call(
        _gdn_kernel,
        grid=(batch, 2, nc),
        in_specs=[pl.BlockSpec((c, hw), lambda b, d, j: (chunk(b, d, j), 0)),
                  pl.BlockSpec((c, hw), lambda b, d, j: (chunk(b, d, j), 1)),
                  pl.BlockSpec((c, hw), lambda b, d, j: (chunk(b, d, j), 2)),
                  pl.BlockSpec((1, hw, c), lambda b, d, j: (chunk(b, d, j), 0, 0)),
                  pl.BlockSpec((c, AB_W), lambda b, d, j: (chunk(b, d, j), 0)),
                  pl.BlockSpec((1, 1, AUX_ROWS, LANES), lambda b, d, j: (chunk(b, d, j), d, 0, 0))],
        out_specs=pl.BlockSpec((1, c, hw), lambda b, d, j: (d, chunk(b, d, j), 0)),
        out_shape=jax.ShapeDtypeStruct((2, t, hw), BF16),
        scratch_shapes=[pltpu.VMEM((GDN_HEADS, GDN_DK, GDN_DV), F32)],
        compiler_params=_cparams("parallel", "parallel", "arbitrary"),
        name="gdn",
    )(qkvn, qkvn, qkvn, kt, gs, aux)


def _mix_kernel(of_ref, ob_ref, z_ref, mg_ref, yb_ref, x_ref, mod_ref, onorm_ref, post1_ref, pre2_ref,
                wa_ref, wb_ref, wo_ref, x1_ref, h2t_ref, gated_scr):
    o = of_ref[0].astype(F32) + ob_ref[0].astype(F32)
    z = z_ref[...].astype(F32)
    for h in range(GDN_HEADS):
        hs = slice(h * GDN_DV, (h + 1) * GDN_DV)
        gated_scr[:, hs] = (_rms(o[:, hs], onorm_ref[...]) * _silu(z[:, hs])).astype(BF16)
    ya = _dot(gated_scr[...], wa_ref[...])
    yb = _dot(yb_ref[...], wb_ref[...])
    d = ya.shape[1]
    mg = mg_ref[...].astype(F32)
    mix = jax.nn.sigmoid(mg[:, :d]) * ya + jax.nn.sigmoid(mg[:, d:]) * yb
    mo = _dot(mix.astype(BF16), wo_ref[...])
    x1 = x_ref[...] + mod_ref[0, 2:3, :] * _rms(mo, post1_ref[...])
    x1_ref[...] = x1
    h2 = _rms(x1, pre2_ref[...]) * (1.0 + mod_ref[0, 4:5, :]) + mod_ref[0, 3:4, :]
    h2t_ref[...] = h2.T.astype(BF16)


def _mix(o2, zin, yb, x2, mod, onorm, post1, pre2, wa, wb, wo, seq, tm):
    t, d = x2.shape
    hw = o2.shape[2]
    tps = seq // tm
    zcol = zin.shape[1] // hw - 1
    mgcol = (zin.shape[1] - hw) // (2 * d) - 1
    full = lambda shape: pl.BlockSpec(shape, lambda i: (0,) * len(shape))
    return pl.pallas_call(
        _mix_kernel,
        grid=(t // tm,),
        in_specs=[pl.BlockSpec((1, tm, hw), lambda i: (0, i, 0)),
                  pl.BlockSpec((1, tm, hw), lambda i: (1, i, 0)),
                  pl.BlockSpec((tm, hw), lambda i: (i, zcol)),
                  pl.BlockSpec((tm, 2 * d), lambda i: (i, mgcol)),
                  pl.BlockSpec((tm, yb.shape[1]), lambda i: (i, 0)),
                  pl.BlockSpec((tm, d), lambda i: (i, 0)),
                  pl.BlockSpec((1, MOD_ROWS, d), lambda i: (i // tps, 0, 0)),
                  full((1, GDN_DV)), full((1, d)), full((1, d)),
                  full(wa.shape), full(wb.shape), full(wo.shape)],
        out_specs=[pl.BlockSpec((tm, d), lambda i: (i, 0)),
                   pl.BlockSpec((d, tm), lambda i: (0, i))],
        out_shape=[jax.ShapeDtypeStruct((t, d), F32),
                   jax.ShapeDtypeStruct((d, t), BF16)],
        scratch_shapes=[pltpu.VMEM((tm, hw), BF16)],
        compiler_params=_cparams("parallel"),
        name="mix",
    )(o2, o2, zin, zin, yb, x2, mod, onorm, post1, pre2, wa, wb, wo)


def _top_vals(s, k):
    vals = []
    cur = s
    for _ in range(k):
        m = jnp.max(cur, axis=0, keepdims=True)
        vals.append(m)
        cur = jnp.where(cur == m, -jnp.inf, cur)
    return vals


def _peerq_kernel(h2t_ref, wq_ref, keys_ref, a_ref, b_ref, thr_ref):
    qt = _dot(wq_ref[...], h2t_ref[...]).astype(BF16)
    dq = keys_ref.shape[2]
    kk = PEER_TOPK
    a_sorted, b_sorted, b_full = [], [], []
    for h in range(PEER_HEADS):
        s0 = _dot(keys_ref[2 * h], qt[(2 * h) * dq:(2 * h + 1) * dq, :])
        s1 = _dot(keys_ref[2 * h + 1], qt[(2 * h + 1) * dq:(2 * h + 2) * dq, :])
        v0 = _top_vals(s0, kk)
        v1 = _top_vals(s1, kk)
        a_ref[h] = jnp.exp(s0 - v0[0])
        b_full.append(jnp.exp(s1 - v1[0]))
        a_sorted.append([jnp.exp(v - v0[0]) for v in v0])
        b_sorted.append([jnp.exp(v - v1[0]) for v in v1])
    a_k = [jnp.concatenate([a_sorted[h][r] for h in range(PEER_HEADS)], axis=0) for r in range(kk)]
    b_k = [jnp.concatenate([b_sorted[h][r] for h in range(PEER_HEADS)], axis=0) for r in range(kk)]
    pairs = [(r0, r1) for r0 in range(kk) for r1 in range(kk // (r0 + 1))]
    cand = [a_k[r0] * b_k[r1] for r0, r1 in pairs]
    cur = list(cand)
    z = jnp.zeros_like(cand[0])
    m = z
    for _ in range(kk):
        m = jnp.maximum(functools.reduce(jnp.maximum, cur), 0.0)
        z = z + m
        cur = [jnp.where(x == m, -1.0, x) for x in cur]
    inv_z = 1.0 / z
    b_n = [b * inv_z for b in b_k]
    thr = functools.reduce(jnp.minimum, [
        jnp.where(cand[n] >= m, a_k[r0] * b_n[r1], jnp.inf) for n, (r0, r1) in enumerate(pairs)])
    thr_ref[...] = thr
    for h in range(PEER_HEADS):
        b_ref[h] = b_full[h] * inv_z[h:h + 1, :]


def _peerq(h2t, wq_t, keys, tm):
    d, t = h2t.shape
    nk = keys.shape[1]
    full = lambda shape: pl.BlockSpec(shape, lambda i: (0,) * len(shape))
    return pl.pallas_call(
        _peerq_kernel,
        grid=(t // tm,),
        in_specs=[pl.BlockSpec((d, tm), lambda i: (0, i)), full(wq_t.shape), full(keys.shape)],
        out_specs=[pl.BlockSpec((PEER_HEADS, nk, tm), lambda i: (0, 0, i)),
                   pl.BlockSpec((PEER_HEADS, nk, tm), lambda i: (0, 0, i)),
                   pl.BlockSpec((PEER_HEADS, tm), lambda i: (0, i))],
        out_shape=[jax.ShapeDtypeStruct((PEER_HEADS, nk, t), F32),
                   jax.ShapeDtypeStruct((PEER_HEADS, nk, t), F32),
                   jax.ShapeDtypeStruct((PEER_HEADS, t), F32)],
        compiler_params=_cparams("parallel"),
        name="peerq",
    )(h2t, wq_t, keys)


def _gelu(x):
    return 0.5 * x * (1.0 + lax.erf(x * (2.0 ** -0.5)))


def _peer_kernel(h2t_ref, u_ref, vt_ref, a_ref, b_ref, thr_ref, x1_ref, mod_ref, post2_ref,
                 out_ref, acc_ref, w2_ref):
    e = pl.program_id(1)
    nk = a_ref.shape[1]
    rows = u_ref.shape[0] // nk

    @pl.when(e == 0)
    def _():
        acc_ref[...] = jnp.zeros_like(acc_ref)

    act = _dot(u_ref[...], h2t_ref[...])
    for r in range(rows):
        i = e * rows + r
        wsum = jnp.zeros((nk, act.shape[1]), F32)
        for h in range(PEER_HEADS):
            p = a_ref[h, pl.ds(i, 1), :] * b_ref[h]
            wsum = wsum + jnp.where(p >= thr_ref[h:h + 1, :], p, 0.0)
        w2_ref[r * nk:(r + 1) * nk, :] = (_gelu(act[r * nk:(r + 1) * nk, :]) * wsum).astype(BF16)
    acc_ref[...] += _dot(vt_ref[...], w2_ref[...])

    @pl.when(e == pl.num_programs(1) - 1)
    def _():
        f = acc_ref[...].T
        out_ref[...] = x1_ref[...] + mod_ref[0, 5:6, :] * _rms(f, post2_ref[...])


def _peer(h2t, u, vt, a, b, thr, x1, mod, post2, seq, tm, te):
    d, t = h2t.shape
    ne = u.shape[0]
    nk = a.shape[1]
    tps = seq // tm
    return pl.pallas_call(
        _peer_kernel,
        grid=(t // tm, ne // te),
        in_specs=[pl.BlockSpec((d, tm), lambda i, e: (0, i)),
                  pl.BlockSpec((te, d), lambda i, e: (e, 0)),
                  pl.BlockSpec((d, te), lambda i, e: (0, e)),
                  pl.BlockSpec((PEER_HEADS, nk, tm), lambda i, e: (0, 0, i)),
                  pl.BlockSpec((PEER_HEADS, nk, tm), lambda i, e: (0, 0, i)),
                  pl.BlockSpec((PEER_HEADS, tm), lambda i, e: (0, i)),
                  pl.BlockSpec((tm, d), lambda i, e: (i, 0)),
                  pl.BlockSpec((1, MOD_ROWS, d), lambda i, e: (i // tps, 0, 0)),
                  pl.BlockSpec((1, d), lambda i, e: (0, 0))],
        out_specs=pl.BlockSpec((tm, d), lambda i, e: (i, 0)),
        out_shape=jax.ShapeDtypeStruct((t, d), F32),
        scratch_shapes=[pltpu.VMEM((d, tm), F32), pltpu.VMEM((te, tm), BF16)],
        compiler_params=_cparams("parallel", "arbitrary"),
        name="peer",
    )(h2t, u, vt, a, b, thr, x1, mod, post2)


def _tile(n, pref):
    return pref if n % pref == 0 else n


def _layer(x, mod, p):
    bsz, seq, d = x.shape
    t = bsz * seq
    x2 = x.reshape(t, d)
    tm = _tile(seq, 512)
    zin, ab = _inproj(x2, mod, p["pre1"], p["w_main"], p["w_ab"], seq, _tile(seq, 1024))
    qkvn, kt, yb, gs, aux = _conv(zin, ab, p["conv_qkv"], p["conv_sc"], p["arow"], p["dtrow"], seq, _tile(seq, 256))
    o2 = _gdn(qkvn, kt, gs, aux, bsz, seq)
    x1, h2t = _mix(o2, zin, yb, x2, mod, p["onorm"], p["post1"], p["pre2"], p["wa"], p["wb"], p["wo"], seq, tm)
    a, b, thr = _peerq(h2t, p["wq_t"], p["keys"], _tile(seq, 256))
    out = _peer(h2t, p["u"], p["vt"], a, b, thr, x1, mod, p["post2"], seq, tm, 512)
    return out.reshape(bsz, seq, d)


def _lane_rows(vals_f, vals_b):
    def half(v):
        return jnp.concatenate([jnp.tile(v, 5), jnp.zeros((AB_W // 2 - 5 * GDN_HEADS,), F32)])
    return jnp.concatenate([half(vals_f), half(vals_b)]).reshape(1, AB_W)


def _prep(l, pre_norm1, post_norm1, pre_norm2, post_norm2, w_in, conv_qkv, a_log, dt_bias, out_norm,
          w_branch_a, conv_sc, w_branch_b, w_out, w_peer_q, peer_sub_keys, peer_u, peer_v):
    d = w_in.shape[1]
    hw = GDN_HEADS * GDN_DK
    qkv_w = 2 * hw + GDN_HEADS * GDN_DV
    z_w = GDN_HEADS * GDN_DV
    nh2 = 2 * GDN_HEADS
    sc_w = 3 * conv_sc.shape[2]
    o_z, o_b, o_a, o_sc = qkv_w, qkv_w + z_w, qkv_w + z_w + nh2, qkv_w + z_w + 2 * nh2
    o_mg = o_sc + sc_w
    w = w_in[l]
    w_main = jnp.concatenate([w[:, :qkv_w], w[:, o_sc:o_mg], w[:, o_mg:], w[:, o_z:o_b]], axis=1).astype(BF16)
    beta = w[:, o_b:o_a]
    alpha = w[:, o_a:o_sc]
    pad = jnp.zeros((d, AB_W // 2 - 5 * GDN_HEADS), F32)

    def half(i):
        al = alpha[:, i * GDN_HEADS:(i + 1) * GDN_HEADS]
        return jnp.concatenate([al, beta[:, i * GDN_HEADS:(i + 1) * GDN_HEADS], al, al, al, pad], axis=1)

    w_ab = jnp.concatenate([half(0), half(1)], axis=1).astype(BF16)
    ea = jnp.exp(a_log[l].astype(F32))
    nkeys, dq = peer_sub_keys.shape[3], peer_sub_keys.shape[4]
    return dict(
        pre1=pre_norm1[l].reshape(1, d), post1=post_norm1[l].reshape(1, d),
        pre2=pre_norm2[l].reshape(1, d), post2=post_norm2[l].reshape(1, d),
        w_main=w_main, w_ab=w_ab, conv_qkv=conv_qkv[l], conv_sc=conv_sc[l],
        arow=_lane_rows(ea[0], ea[1]), dtrow=_lane_rows(dt_bias[l, 0], dt_bias[l, 1]),
        onorm=out_norm[l].reshape(1, GDN_DV),
        wa=w_branch_a[l].astype(BF16), wb=w_branch_b[l].astype(BF16), wo=w_out[l].astype(BF16),
        wq_t=w_peer_q[l].T.astype(BF16),
        keys=peer_sub_keys[l].reshape(PEER_HEADS * 2, nkeys, dq).astype(BF16),
        u=peer_u[l].astype(BF16), vt=peer_v[l].T.astype(BF16))


def kernel(x_prompt, x_sample, c_prompt, c_sample, pre_norm1, post_norm1, pre_norm2, post_norm2, w_ada, b_ada, w_in, conv_qkv, a_log, dt_bias, out_norm, w_branch_a, conv_sc, w_branch_b, w_out, w_peer_q, peer_sub_keys, peer_u, peer_v):
    depth = w_in.shape[0]
    d = x_prompt.shape[2]
    nb = x_prompt.shape[0]
    xs = [x_prompt, x_sample]
    c_all = jnp.concatenate([c_prompt, c_sample], axis=0)
    for l in range(depth):
        p = _prep(l, pre_norm1, post_norm1, pre_norm2, post_norm2, w_in, conv_qkv, a_log, dt_bias, out_norm,
                  w_branch_a, conv_sc, w_branch_b, w_out, w_peer_q, peer_sub_keys, peer_u, peer_v)
        mod = _ada(c_all, w_ada[l], b_ada[l]).reshape(c_all.shape[0], N_MOD, d)
        mod = jnp.pad(mod, ((0, 0), (0, MOD_ROWS - N_MOD), (0, 0)))
        xs = [_layer(xs[0], mod[:nb], p), _layer(xs[1], mod[nb:], p)]
    return (xs[0], xs[1])
```

```python
import functools
import math

import jax
import jax.numpy as jnp
from jax import lax
from jax.experimental import pallas as pl
from jax.experimental.pallas import tpu as pltpu

F32 = jnp.float32
BF16 = jnp.bfloat16

NORM_EPS = 1e-6
N_MOD = 6
MOD_ROWS = 8
GDN_HEADS = 8
GDN_DK = 128
GDN_DV = 128
GDN_CHUNK = 64
PEER_HEADS = 8
PEER_NKEYS = 128
PEER_TOPK = 16
LANES = 128
HALO = 16
AB_W = 128
AUX_ROWS = 24
NEG_BIG = -1e30
VMEM_LIMIT = 56 * 1024 * 1024


def _silu(x):
    return x * jax.nn.sigmoid(x)


def _softplus(x):
    return jnp.maximum(x, 0.0) + jnp.log1p(jnp.exp(-jnp.abs(x)))


def _rms(x, g):
    return x * lax.rsqrt(jnp.mean(x * x, axis=-1, keepdims=True) + NORM_EPS) * g


def _dot(a, b):
    return jnp.dot(a, b, preferred_element_type=F32)


def _cparams(*sem):
    return pltpu.CompilerParams(dimension_semantics=sem, vmem_limit_bytes=VMEM_LIMIT)


def _ada_kernel(c_ref, w_ref, b_ref, o_ref):
    c = c_ref[...]
    o_ref[...] = _dot(_silu(c).astype(BF16), w_ref[...].astype(BF16)) + b_ref[...]


def _ada(c, w, b, tn=1536):
    m, d = c.shape
    n = w.shape[1]
    return pl.pallas_call(
        _ada_kernel,
        grid=(n // tn,),
        in_specs=[pl.BlockSpec((m, d), lambda j: (0, 0)),
                  pl.BlockSpec((d, tn), lambda j: (0, j)),
                  pl.BlockSpec((1, tn), lambda j: (0, j))],
        out_specs=pl.BlockSpec((m, tn), lambda j: (0, j)),
        out_shape=jax.ShapeDtypeStruct((m, n), F32),
        compiler_params=_cparams("parallel"),
        name="ada",
    )(c, w, b.reshape(1, n))


def _inproj_kernel(x_ref, mod_ref, g_ref, w_ref, wab_ref, zin_ref, ab_ref, h_scr):
    @pl.when(pl.program_id(1) == 0)
    def _():
        y = _rms(x_ref[...], g_ref[...])
        h = y * (1.0 + mod_ref[0, 1:2, :]) + mod_ref[0, 0:1, :]
        hb = h.astype(BF16)
        h_scr[...] = hb
        ab_ref[...] = _dot(hb, wab_ref[...])

    zin_ref[...] = _dot(h_scr[...], w_ref[...]).astype(BF16)


def _inproj(x2, mod, g, w_main, w_ab, seq, tm, tn=1024):
    t, d = x2.shape
    n = w_main.shape[1]
    tps = seq // tm
    return pl.pallas_call(
        _inproj_kernel,
        grid=(t // tm, n // tn),
        in_specs=[pl.BlockSpec((tm, d), lambda i, j: (i, 0)),
                  pl.BlockSpec((1, MOD_ROWS, d), lambda i, j: (i // tps, 0, 0)),
                  pl.BlockSpec((1, d), lambda i, j: (0, 0)),
                  pl.BlockSpec((d, tn), lambda i, j: (0, j)),
                  pl.BlockSpec((d, AB_W), lambda i, j: (0, 0))],
        out_specs=[pl.BlockSpec((tm, tn), lambda i, j: (i, j)),
                   pl.BlockSpec((tm, AB_W), lambda i, j: (i, 0))],
        out_shape=[jax.ShapeDtypeStruct((t, n), BF16),
                   jax.ShapeDtypeStruct((t, AB_W), F32)],
        scratch_shapes=[pltpu.VMEM((tm, d), BF16)],
        compiler_params=_cparams("parallel", "arbitrary"),
        name="inproj",
    )(x2, mod, g, w_main, w_ab)


def _shift_rows(x, prev_row, next_row):
    n = x.shape[0]
    rows = lax.broadcasted_iota(jnp.int32, (n, 1), 0)
    xm = jnp.where(rows == 0, prev_row, pltpu.roll(x, 1, axis=0))
    xp = jnp.where(rows == n - 1, next_row, pltpu.roll(x, n - 1, axis=0))
    return xm, xp


def _conv_kernel(qkv_ref, qp_ref, qn_ref, sc_ref, sp_ref, sn_ref, ab_ref, cq_ref, cs_ref,
                 arow_ref, dtrow_ref, qkvn_ref, kt_ref, yb_ref, gs_ref, aux_ref, *, tiles_per_seq):
    i = pl.program_id(0)
    tb = qkv_ref.shape[0]
    nch = tb // GDN_CHUNK
    hw = GDN_HEADS * GDN_DK
    keep_prev = ((i % tiles_per_seq) != 0).astype(F32)
    keep_next = ((i % tiles_per_seq) != tiles_per_seq - 1).astype(F32)

    x = qkv_ref[...].astype(F32)
    xm, xp = _shift_rows(x, qp_ref[HALO - 1:HALO, :].astype(F32) * keep_prev,
                         qn_ref[0:1, :].astype(F32) * keep_next)
    y = _silu(cq_ref[0:1, :] * xm + cq_ref[1:2, :] * x + cq_ref[2:3, :] * xp)
    for hh in range(2 * GDN_HEADS):
        seg = y[:, hh * GDN_DK:(hh + 1) * GDN_DK]
        nrm = seg * lax.rsqrt(jnp.sum(seg * seg, axis=-1, keepdims=True) + NORM_EPS)
        if hh < GDN_HEADS:
            nrm = nrm * (GDN_DK ** -0.5)
        qkvn_ref[:, hh * GDN_DK:(hh + 1) * GDN_DK] = nrm.astype(BF16)
        if hh >= GDN_HEADS:
            kt = nrm.T
            for c in range(nch):
                kt_ref[c, (hh - GDN_HEADS) * GDN_DK:(hh - GDN_HEADS + 1) * GDN_DK, :] = (
                    kt[:, c * GDN_CHUNK:(c + 1) * GDN_CHUNK].astype(BF16))
    qkvn_ref[:, 2 * hw:] = y[:, 2 * hw:].astype(BF16)

    w = sc_ref.shape[1] // 3
    s = sc_ref[...].astype(F32)
    p = s[:, 2 * w:] * s[:, :w]
    sp = sp_ref[HALO - 1:HALO, :].astype(F32)
    sn = sn_ref[0:1, :].astype(F32)
    pm, pp = _shift_rows(p, sp[:, 2 * w:] * sp[:, :w] * keep_prev, sn[:, 2 * w:] * sn[:, :w] * keep_next)
    yb_ref[...] = (s[:, w:2 * w] * (cs_ref[0:1, :] * pm + cs_ref[1:2, :] * p + cs_ref[2:3, :] * pp)).astype(BF16)

    ab = ab_ref[...]
    g = -arow_ref[...] * _softplus(ab + dtrow_ref[...])
    rows = lax.broadcasted_iota(jnp.int32, (tb, 1), 0) % GDN_CHUNK
    lane = lax.broadcasted_iota(jnp.int32, (1, AB_W), 1)
    pre, suf = g, g
    sh = 1
    while sh < GDN_CHUNK:
        pre = pre + jnp.where(rows >= sh, pltpu.roll(pre, sh, axis=0), 0.0)
        suf = suf + jnp.where(rows < GDN_CHUNK - sh, pltpu.roll(suf, tb - sh, axis=0), 0.0)
        sh *= 2
    tot = pre + suf - g
    gc = jnp.where(lane >= AB_W // 2, suf, pre)
    grp = (lane % (AB_W // 2)) // GDN_HEADS
    gs = jnp.where(grp == 0, gc,
                   jnp.where(grp == 1, jax.nn.sigmoid(ab),
                             jnp.where(grp == 2, jnp.exp(gc),
                                       jnp.where(grp == 3, jnp.exp(tot - gc), jnp.exp(tot)))))
    gs_ref[...] = gs
    gst = gs.T
    for c in range(nch):
        sl = slice(c * GDN_CHUNK, (c + 1) * GDN_CHUNK)
        for d in range(2):
            base = d * (AB_W // 2)
            blk = jnp.zeros((AUX_ROWS, LANES), F32)
            aux_ref[c, d] = blk
            aux_ref[c, d, 0:8, 0:GDN_CHUNK] = gst[base:base + 8, sl]
            aux_ref[c, d, 8:16, 0:GDN_CHUNK] = gst[base + 24:base + 32, sl]
            aux_ref[c, d, 16:24, :] = jnp.broadcast_to(
                gst[base + 32:base + 40, c * GDN_CHUNK:c * GDN_CHUNK + 1], (8, LANES))


def _conv(zin, ab, conv_qkv, conv_sc, arow, dtrow, seq, tb):
    t = zin.shape[0]
    qw = conv_qkv.shape[1]
    sw = conv_sc.shape[1]
    hw = GDN_HEADS * GDN_DK
    assert qw == 3 * hw and zin.shape[1] % qw == 0
    nt = t // tb
    tps = seq // tb
    hb = tb // HALO
    nhalo = t // HALO
    prev = lambda c: (lambda i: (jnp.maximum(i * hb - 1, 0), c))
    nxt = lambda c: (lambda i: (jnp.minimum((i + 1) * hb, nhalo - 1), c))
    nch = tb // GDN_CHUNK
    return pl.pallas_call(
        functools.partial(_conv_kernel, tiles_per_seq=tps),
        grid=(nt,),
        in_specs=[pl.BlockSpec((tb, qw), lambda i: (i, 0)),
                  pl.BlockSpec((HALO, qw), prev(0)),
                  pl.BlockSpec((HALO, qw), nxt(0)),
                  pl.BlockSpec((tb, qw), lambda i: (i, 1)),
                  pl.BlockSpec((HALO, qw), prev(1)),
                  pl.BlockSpec((HALO, qw), nxt(1)),
                  pl.BlockSpec((tb, AB_W), lambda i: (i, 0)),
                  pl.BlockSpec((3, qw), lambda i: (0, 0)),
                  pl.BlockSpec((3, sw), lambda i: (0, 0)),
                  pl.BlockSpec((1, AB_W), lambda i: (0, 0)),
                  pl.BlockSpec((1, AB_W), lambda i: (0, 0))],
        out_specs=[pl.BlockSpec((tb, qw), lambda i: (i, 0)),
                   pl.BlockSpec((nch, hw, GDN_CHUNK), lambda i: (i, 0, 0)),
                   pl.BlockSpec((tb, sw), lambda i: (i, 0)),
                   pl.BlockSpec((tb, AB_W), lambda i: (i, 0)),
                   pl.BlockSpec((nch, 2, AUX_ROWS, LANES), lambda i: (i, 0, 0, 0))],
        out_shape=[jax.ShapeDtypeStruct((t, qw), BF16),
                   jax.ShapeDtypeStruct((t // GDN_CHUNK, hw, GDN_CHUNK), BF16),
                   jax.ShapeDtypeStruct((t, sw), BF16),
                   jax.ShapeDtypeStruct((t, AB_W), F32),
                   jax.ShapeDtypeStruct((t // GDN_CHUNK, 2, AUX_ROWS, LANES), F32)],
        compiler_params=_cparams("parallel"),
        name="conv",
    )(zin, zin, zin, zin, zin, zin, ab, conv_qkv, conv_sc, arow, dtrow)


def _gdn_kernel(qf_ref, kf_ref, vf_ref, ktf_ref, gsf_ref, auxf_ref,
                qb_ref, kb_ref, vb_ref, ktb_ref, gsb_ref, auxb_ref, of_ref, ob_ref, s_scr):
    @pl.when(pl.program_id(1) == 0)
    def _():
        s_scr[...] = jnp.zeros_like(s_scr)

    c = GDN_CHUNK
    nh = GDN_HEADS
    ri = lax.broadcasted_iota(jnp.int32, (c, c), 0)
    ci = lax.broadcasted_iota(jnp.int32, (c, c), 1)
    incl_d = (ri >= ci, ri <= ci)
    strict_d = (ri > ci, ri < ci)
    refs = ((qf_ref, kf_ref, vf_ref, ktf_ref, gsf_ref, auxf_ref, of_ref),
            (qb_ref, kb_ref, vb_ref, ktb_ref, gsb_ref, auxb_ref, ob_ref))
    units = [(d, h) for d in range(2) for h in range(nh)]
    every = lambda f: [f(n, d, h) for n, (d, h) in enumerate(units)]
    hsl = lambda h: slice(h * GDN_DK, (h + 1) * GDN_DK)
    col = lambda d, j: slice(d * (AB_W // 2) + j, d * (AB_W // 2) + j + 1)

    gsv = [gsf_ref[...], gsb_ref[...]]
    q = every(lambda n, d, h: refs[d][0][:, hsl(h)])
    k = every(lambda n, d, h: refs[d][1][:, hsl(h)].astype(F32))
    v = every(lambda n, d, h: refs[d][2][:, hsl(h)].astype(F32))
    kt = every(lambda n, d, h: refs[d][3][0, hsl(h), :])
    gcol = every(lambda n, d, h: gsv[d][:, col(d, h)])
    beta = every(lambda n, d, h: gsv[d][:, col(d, nh + h)])
    egc = every(lambda n, d, h: gsv[d][:, col(d, 2 * nh + h)])
    grow = every(lambda n, d, h: refs[d][5][0, 0, h:h + 1, 0:c])
    kdrow = every(lambda n, d, h: refs[d][5][0, 0, nh + h:nh + h + 1, 0:c])
    etot = every(lambda n, d, h: refs[d][5][0, 0, 2 * nh + h:2 * nh + h + 1, :])

    decay = every(lambda n, d, h: jnp.exp(jnp.where(incl_d[d], gcol[n] - grow[n], NEG_BIG)))
    kb = every(lambda n, d, h: k[n] * beta[n])
    gram = every(lambda n, d, h: _dot(jnp.concatenate([kb[n].astype(BF16), q[n]], axis=0), kt[n]))
    low = every(lambda n, d, h: jnp.where(strict_d[d], gram[n][:c] * decay[n], 0.0))
    attn = every(lambda n, d, h: jnp.where(incl_d[d], gram[n][c:] * decay[n], 0.0).astype(BF16))

    x = every(lambda n, d, h: jnp.concatenate([v[n] * beta[n], kb[n] * egc[n]], axis=-1))
    lp = [l.astype(BF16) for l in low]
    x = every(lambda n, d, h: x[n] - _dot(lp[n], x[n].astype(BF16)))
    p = 2
    while p < c:
        lp = [_dot(l, l).astype(BF16) for l in lp]
        x = every(lambda n, d, h: x[n] + _dot(lp[n], x[n].astype(BF16)))
        p *= 2

    s = [s_scr[n] for n in range(len(units))]
    sb = [t.astype(BF16) for t in s]
    qg = every(lambda n, d, h: (q[n].astype(F32) * egc[n]).astype(BF16))
    ws = every(lambda n, d, h: _dot(jnp.concatenate([x[n][:, GDN_DV:].astype(BF16), qg[n]], axis=0), sb[n]))
    vnb = every(lambda n, d, h: (x[n][:, :GDN_DV] - ws[n][:c]).astype(BF16))
    o = every(lambda n, d, h: ws[n][c:] + _dot(attn[n], vnb[n]))
    for n, (d, h) in enumerate(units):
        refs[d][6][:, hsl(h)] = o[n].astype(BF16)
    ktd = every(lambda n, d, h: (kt[n].astype(F32) * kdrow[n]).astype(BF16))
    for n in range(len(units)):
        s_scr[n] = s[n] * etot[n] + _dot(ktd[n], vnb[n])


def _gdn(qkvn, kt, gs, aux, batch, seq):
    t = qkvn.shape[0]
    hw = GDN_HEADS * GDN_DK
    c = GDN_CHUNK
    nc = seq // c
    fwd = lambda b, j: b * nc + j
    bwd = lambda b, j: b * nc + nc - 1 - j

    def specs(ch, d):
        return [pl.BlockSpec((c, hw), lambda b, j: (ch(b, j), 0)),
                pl.BlockSpec((c, hw), lambda b, j: (ch(b, j), 1)),
                pl.BlockSpec((c, hw), lambda b, j: (ch(b, j), 2)),
                pl.BlockSpec((1, hw, c), lambda b, j: (ch(b, j), 0, 0)),
                pl.BlockSpec((c, AB_W), lambda b, j: (ch(b, j), 0)),
                pl.BlockSpec((1, 1, AUX_ROWS, LANES), lambda b, j: (ch(b, j), d, 0, 0))]

    return pl.pallas_call(
        _gdn_kernel,
        grid=(batch, nc),
        in_specs=specs(fwd, 0) + specs(bwd, 1),
        out_specs=[pl.BlockSpec((c, hw), lambda b, j: (fwd(b, j), 0)),
                   pl.BlockSpec((c, hw), lambda b, j: (bwd(b, j), 0))],
        out_shape=[jax.ShapeDtypeStruct((t, hw), BF16), jax.ShapeDtypeStruct((t, hw), BF16)],
        scratch_shapes=[pltpu.VMEM((2 * GDN_HEADS, GDN_DK, GDN_DV), F32)],
        compiler_params=_cparams("parallel", "arbitrary"),
        name="gdn",
    )(qkvn, qkvn, qkvn, kt, gs, aux, qkvn, qkvn, qkvn, kt, gs, aux)


def _mix_kernel(of_ref, ob_ref, z_ref, mg_ref, yb_ref, x_ref, mod_ref, onorm_ref, post1_ref, pre2_ref,
                wa_ref, wb_ref, wo_ref, x1_ref, h2t_ref, gated_scr):
    o = of_ref[...].astype(F32) + ob_ref[...].astype(F32)
    z = z_ref[...].astype(F32)
    for h in range(GDN_HEADS):
        hs = slice(h * GDN_DV, (h + 1) * GDN_DV)
        gated_scr[:, hs] = (_rms(o[:, hs], onorm_ref[...]) * _silu(z[:, hs])).astype(BF16)
    ya = _dot(gated_scr[...], wa_ref[...])
    yb = _dot(yb_ref[...], wb_ref[...])
    d = ya.shape[1]
    mg = mg_ref[...].astype(F32)
    mix = jax.nn.sigmoid(mg[:, :d]) * ya + jax.nn.sigmoid(mg[:, d:]) * yb
    mo = _dot(mix.astype(BF16), wo_ref[...])
    x1 = x_ref[...] + mod_ref[0, 2:3, :] * _rms(mo, post1_ref[...])
    x1_ref[...] = x1
    h2 = _rms(x1, pre2_ref[...]) * (1.0 + mod_ref[0, 4:5, :]) + mod_ref[0, 3:4, :]
    h2t_ref[...] = h2.T.astype(BF16)


def _mix(o_f, o_b, zin, yb, x2, mod, onorm, post1, pre2, wa, wb, wo, seq, tm):
    t, d = x2.shape
    hw = o_f.shape[1]
    tps = seq // tm
    zcol = zin.shape[1] // hw - 1
    mgcol = (zin.shape[1] - hw) // (2 * d) - 1
    full = lambda shape: pl.BlockSpec(shape, lambda i: (0,) * len(shape))
    return pl.pallas_call(
        _mix_kernel,
        grid=(t // tm,),
        in_specs=[pl.BlockSpec((tm, hw), lambda i: (i, 0)),
                  pl.BlockSpec((tm, hw), lambda i: (i, 0)),
                  pl.BlockSpec((tm, hw), lambda i: (i, zcol)),
                  pl.BlockSpec((tm, 2 * d), lambda i: (i, mgcol)),
                  pl.BlockSpec((tm, yb.shape[1]), lambda i: (i, 0)),
                  pl.BlockSpec((tm, d), lambda i: (i, 0)),
                  pl.BlockSpec((1, MOD_ROWS, d), lambda i: (i // tps, 0, 0)),
                  full((1, GDN_DV)), full((1, d)), full((1, d)),
                  full(wa.shape), full(wb.shape), full(wo.shape)],
        out_specs=[pl.BlockSpec((tm, d), lambda i: (i, 0)),
                   pl.BlockSpec((d, tm), lambda i: (0, i))],
        out_shape=[jax.ShapeDtypeStruct((t, d), F32),
                   jax.ShapeDtypeStruct((d, t), BF16)],
        scratch_shapes=[pltpu.VMEM((tm, hw), BF16)],
        compiler_params=_cparams("parallel"),
        name="mix",
    )(o_f, o_b, zin, zin, yb, x2, mod, onorm, post1, pre2, wa, wb, wo)


def _top_vals(s, k):
    vals = []
    cur = s
    for _ in range(k):
        m = jnp.max(cur, axis=0, keepdims=True)
        vals.append(m)
        cur = jnp.where(cur == m, -jnp.inf, cur)
    return vals


def _peerq_kernel(h2t_ref, wq_ref, keys_ref, a_ref, b_ref, thr_ref):
    qt = _dot(wq_ref[...], h2t_ref[...]).astype(BF16)
    dq = keys_ref.shape[2]
    kk = PEER_TOPK
    a_sorted, b_sorted, b_full = [], [], []
    for h in range(PEER_HEADS):
        s0 = _dot(keys_ref[2 * h], qt[(2 * h) * dq:(2 * h + 1) * dq, :])
        s1 = _dot(keys_ref[2 * h + 1], qt[(2 * h + 1) * dq:(2 * h + 2) * dq, :])
        v0 = _top_vals(s0, kk)
        v1 = _top_vals(s1, kk)
        a_ref[h] = jnp.exp(s0 - v0[0])
        b_full.append(jnp.exp(s1 - v1[0]))
        a_sorted.append([jnp.exp(v - v0[0]) for v in v0])
        b_sorted.append([jnp.exp(v - v1[0]) for v in v1])
    a_k = [jnp.concatenate([a_sorted[h][r] for h in range(PEER_HEADS)], axis=0) for r in range(kk)]
    b_k = [jnp.concatenate([b_sorted[h][r] for h in range(PEER_HEADS)], axis=0) for r in range(kk)]
    pairs = [(r0, r1) for r0 in range(kk) for r1 in range(kk // (r0 + 1))]
    cand = [a_k[r0] * b_k[r1] for r0, r1 in pairs]
    cur = list(cand)
    z = jnp.zeros_like(cand[0])
    m = z
    for _ in range(kk):
        m = jnp.maximum(functools.reduce(jnp.maximum, cur), 0.0)
        z = z + m
        cur = [jnp.where(x == m, -1.0, x) for x in cur]
    inv_z = 1.0 / z
    b_n = [b * inv_z for b in b_k]
    thr = functools.reduce(jnp.minimum, [
        jnp.where(cand[n] >= m, a_k[r0] * b_n[r1], jnp.inf) for n, (r0, r1) in enumerate(pairs)])
    thr_ref[...] = thr
    for h in range(PEER_HEADS):
        b_ref[h] = b_full[h] * inv_z[h:h + 1, :]


def _peerq(h2t, wq_t, keys, tm):
    d, t = h2t.shape
    nk = keys.shape[1]
    full = lambda shape: pl.BlockSpec(shape, lambda i: (0,) * len(shape))
    return pl.pallas_call(
        _peerq_kernel,
        grid=(t // tm,),
        in_specs=[pl.BlockSpec((d, tm), lambda i: (0, i)), full(wq_t.shape), full(keys.shape)],
        out_specs=[pl.BlockSpec((PEER_HEADS, nk, tm), lambda i: (0, 0, i)),
                   pl.BlockSpec((PEER_HEADS, nk, tm), lambda i: (0, 0, i)),
                   pl.BlockSpec((PEER_HEADS, tm), lambda i: (0, i))],
        out_shape=[jax.ShapeDtypeStruct((PEER_HEADS, nk, t), F32),
                   jax.ShapeDtypeStruct((PEER_HEADS, nk, t), F32),
                   jax.ShapeDtypeStruct((PEER_HEADS, t), F32)],
        compiler_params=_cparams("parallel"),
        name="peerq",
    )(h2t, wq_t, keys)


def _gelu(x):
    return 0.5 * x * (1.0 + lax.erf(x * (2.0 ** -0.5)))


def _peer_kernel(h2t_ref, u_ref, vt_ref, a_ref, b_ref, thr_ref, x1_ref, mod_ref, post2_ref,
                 out_ref, acc_ref, w2_ref):
    e = pl.program_id(1)
    nk = a_ref.shape[1]
    rows = u_ref.shape[0] // nk

    @pl.when(e == 0)
    def _():
        acc_ref[...] = jnp.zeros_like(acc_ref)

    act = _dot(u_ref[...], h2t_ref[...])
    for r in range(rows):
        i = e * rows + r
        wsum = jnp.zeros((nk, act.shape[1]), F32)
        for h in range(PEER_HEADS):
            p = a_ref[h, pl.ds(i, 1), :] * b_ref[h]
            wsum = wsum + jnp.where(p >= thr_ref[h:h + 1, :], p, 0.0)
        w2_ref[r * nk:(r + 1) * nk, :] = (_gelu(act[r * nk:(r + 1) * nk, :]) * wsum).astype(BF16)
    acc_ref[...] += _dot(vt_ref[...], w2_ref[...])

    @pl.when(e == pl.num_programs(1) - 1)
    def _():
        f = acc_ref[...].T
        out_ref[...] = x1_ref[...] + mod_ref[0, 5:6, :] * _rms(f, post2_ref[...])


def _peer(h2t, u, vt, a, b, thr, x1, mod, post2, seq, tm, te):
    d, t = h2t.shape
    ne = u.shape[0]
    nk = a.shape[1]
    tps = seq // tm
    return pl.pallas_call(
        _peer_kernel,
        grid=(t // tm, ne // te),
        in_specs=[pl.BlockSpec((d, tm), lambda i, e: (0, i)),
                  pl.BlockSpec((te, d), lambda i, e: (e, 0)),
                  pl.BlockSpec((d, te), lambda i, e: (0, e)),
                  pl.BlockSpec((PEER_HEADS, nk, tm), lambda i, e: (0, 0, i)),
                  pl.BlockSpec((PEER_HEADS, nk, tm), lambda i, e: (0, 0, i)),
                  pl.BlockSpec((PEER_HEADS, tm), lambda i, e: (0, i)),
                  pl.BlockSpec((tm, d), lambda i, e: (i, 0)),
                  pl.BlockSpec((1, MOD_ROWS, d), lambda i, e: (i // tps, 0, 0)),
                  pl.BlockSpec((1, d), lambda i, e: (0, 0))],
        out_specs=pl.BlockSpec((tm, d), lambda i, e: (i, 0)),
        out_shape=jax.ShapeDtypeStruct((t, d), F32),
        scratch_shapes=[pltpu.VMEM((d, tm), F32), pltpu.VMEM((te, tm), BF16)],
        compiler_params=_cparams("parallel", "arbitrary"),
        name="peer",
    )(h2t, u, vt, a, b, thr, x1, mod, post2)


def _tile(n, pref):
    return pref if n % pref == 0 else n


def _layer(x, mod, p):
    bsz, seq, d = x.shape
    t = bsz * seq
    x2 = x.reshape(t, d)
    tm = _tile(seq, 512)
    zin, ab = _inproj(x2, mod, p["pre1"], p["w_main"], p["w_ab"], seq, _tile(seq, 1024))
    qkvn, kt, yb, gs, aux = _conv(zin, ab, p["conv_qkv"], p["conv_sc"], p["arow"], p["dtrow"], seq, _tile(seq, 256))
    o_f, o_b = _gdn(qkvn, kt, gs, aux, bsz, seq)
    x1, h2t = _mix(o_f, o_b, zin, yb, x2, mod, p["onorm"], p["post1"], p["pre2"], p["wa"], p["wb"], p["wo"], seq, tm)
    a, b, thr = _peerq(h2t, p["wq_t"], p["keys"], _tile(seq, 256))
    out = _peer(h2t, p["u"], p["vt"], a, b, thr, x1, mod, p["post2"], seq, tm, 512)
    return out.reshape(bsz, seq, d)


def _lane_rows(vals_f, vals_b):
    def half(v):
        return jnp.concatenate([jnp.tile(v, 5), jnp.zeros((AB_W // 2 - 5 * GDN_HEADS,), F32)])
    return jnp.concatenate([half(vals_f), half(vals_b)]).reshape(1, AB_W)


def _prep(l, pre_norm1, post_norm1, pre_norm2, post_norm2, w_in, conv_qkv, a_log, dt_bias, out_norm,
          w_branch_a, conv_sc, w_branch_b, w_out, w_peer_q, peer_sub_keys, peer_u, peer_v):
    d = w_in.shape[1]
    hw = GDN_HEADS * GDN_DK
    qkv_w = 2 * hw + GDN_HEADS * GDN_DV
    z_w = GDN_HEADS * GDN_DV
    nh2 = 2 * GDN_HEADS
    sc_w = 3 * conv_sc.shape[2]
    o_z, o_b, o_a, o_sc = qkv_w, qkv_w + z_w, qkv_w + z_w + nh2, qkv_w + z_w + 2 * nh2
    o_mg = o_sc + sc_w
    w = w_in[l]
    w_main = jnp.concatenate([w[:, :qkv_w], w[:, o_sc:o_mg], w[:, o_mg:], w[:, o_z:o_b]], axis=1).astype(BF16)
    beta = w[:, o_b:o_a]
    alpha = w[:, o_a:o_sc]
    pad = jnp.zeros((d, AB_W // 2 - 5 * GDN_HEADS), F32)

    def half(i):
        al = alpha[:, i * GDN_HEADS:(i + 1) * GDN_HEADS]
        return jnp.concatenate([al, beta[:, i * GDN_HEADS:(i + 1) * GDN_HEADS], al, al, al, pad], axis=1)

    w_ab = jnp.concatenate([half(0), half(1)], axis=1).astype(BF16)
    ea = jnp.exp(a_log[l].astype(F32))
    nkeys, dq = peer_sub_keys.shape[3], peer_sub_keys.shape[4]
    return dict(
        pre1=pre_norm1[l].reshape(1, d), post1=post_norm1[l].reshape(1, d),
        pre2=pre_norm2[l].reshape(1, d), post2=post_norm2[l].reshape(1, d),
        w_main=w_main, w_ab=w_ab, conv_qkv=conv_qkv[l], conv_sc=conv_sc[l],
        arow=_lane_rows(ea[0], ea[1]), dtrow=_lane_rows(dt_bias[l, 0], dt_bias[l, 1]),
        onorm=out_norm[l].reshape(1, GDN_DV),
        wa=w_branch_a[l].astype(BF16), wb=w_branch_b[l].astype(BF16), wo=w_out[l].astype(BF16),
        wq_t=w_peer_q[l].T.astype(BF16),
        keys=peer_sub_keys[l].reshape(PEER_HEADS * 2, nkeys, dq).astype(BF16),
        u=peer_u[l].astype(BF16), vt=peer_v[l].T.astype(BF16))


def kernel(x_prompt, x_sample, c_prompt, c_sample, pre_norm1, post_norm1, pre_norm2, post_norm2, w_ada, b_ada, w_in, conv_qkv, a_log, dt_bias, out_norm, w_branch_a, conv_sc, w_branch_b, w_out, w_peer_q, peer_sub_keys, peer_u, peer_v):
    depth = w_in.shape[0]
    d = x_prompt.shape[2]
    nb = x_prompt.shape[0]
    xs = [x_prompt, x_sample]
    c_all = jnp.concatenate([c_prompt, c_sample], axis=0)
    for l in range(depth):
        p = _prep(l, pre_norm1, post_norm1, pre_norm2, post_norm2, w_in, conv_qkv, a_log, dt_bias, out_norm,
                  w_branch_a, conv_sc, w_branch_b, w_out, w_peer_q, peer_sub_keys, peer_u, peer_v)
        mod = _ada(c_all, w_ada[l], b_ada[l]).reshape(c_all.shape[0], N_MOD, d)
        mod = jnp.pad(mod, ((0, 0), (0, MOD_ROWS - N_MOD), (0, 0)))
        xs = [_layer(xs[0], mod[:nb], p), _layer(xs[1], mod[nb:], p)]
    return (xs[0], xs[1])
```

```python
import functools
import math

import jax
import jax.numpy as jnp
from jax import lax
from jax.experimental import pallas as pl
from jax.experimental.pallas import tpu as pltpu

F32 = jnp.float32
BF16 = jnp.bfloat16

NORM_EPS = 1e-6
N_MOD = 6
MOD_ROWS = 8
GDN_HEADS = 8
GDN_DK = 128
GDN_DV = 128
GDN_CHUNK = 64
PEER_HEADS = 8
PEER_NKEYS = 128
PEER_TOPK = 16
LANES = 128
HALO = 16
AB_W = 128
AUX_ROWS = 24
NEG_BIG = -1e30
VMEM_LIMIT = 56 * 1024 * 1024


def _silu(x):
    return x * jax.nn.sigmoid(x)


def _softplus(x):
    return jnp.maximum(x, 0.0) + jnp.log1p(jnp.exp(-jnp.abs(x)))


def _rms(x, g):
    return x * lax.rsqrt(jnp.mean(x * x, axis=-1, keepdims=True) + NORM_EPS) * g


def _dot(a, b):
    return jnp.dot(a, b, preferred_element_type=F32)


def _cparams(*sem):
    return pltpu.CompilerParams(dimension_semantics=sem, vmem_limit_bytes=VMEM_LIMIT)


def _ada_kernel(c_ref, w_ref, b_ref, o_ref):
    c = c_ref[...]
    o_ref[...] = _dot(_silu(c).astype(BF16), w_ref[...].astype(BF16)) + b_ref[...]


def _ada(c, w, b, tn=1536):
    m, d = c.shape
    n = w.shape[1]
    return pl.pallas_call(
        _ada_kernel,
        grid=(n // tn,),
        in_specs=[pl.BlockSpec((m, d), lambda j: (0, 0)),
                  pl.BlockSpec((d, tn), lambda j: (0, j)),
                  pl.BlockSpec((1, tn), lambda j: (0, j))],
        out_specs=pl.BlockSpec((m, tn), lambda j: (0, j)),
        out_shape=jax.ShapeDtypeStruct((m, n), F32),
        compiler_params=_cparams("parallel"),
        name="ada",
    )(c, w, b.reshape(1, n))


def _inproj_kernel(x_ref, mod_ref, g_ref, w_ref, wab_ref, zin_ref, ab_ref, h_scr):
    @pl.when(pl.program_id(1) == 0)
    def _():
        y = _rms(x_ref[...], g_ref[...])
        h = y * (1.0 + mod_ref[0, 1:2, :]) + mod_ref[0, 0:1, :]
        hb = h.astype(BF16)
        h_scr[...] = hb
        ab_ref[...] = _dot(hb, wab_ref[...])

    zin_ref[...] = _dot(h_scr[...], w_ref[...]).astype(BF16)


def _inproj(x2, mod, g, w_main, w_ab, seq, tm, tn=1024):
    t, d = x2.shape
    n = w_main.shape[1]
    tps = seq // tm
    return pl.pallas_call(
        _inproj_kernel,
        grid=(t // tm, n // tn),
        in_specs=[pl.BlockSpec((tm, d), lambda i, j: (i, 0)),
                  pl.BlockSpec((1, MOD_ROWS, d), lambda i, j: (i // tps, 0, 0)),
                  pl.BlockSpec((1, d), lambda i, j: (0, 0)),
                  pl.BlockSpec((d, tn), lambda i, j: (0, j)),
                  pl.BlockSpec((d, AB_W), lambda i, j: (0, 0))],
        out_specs=[pl.BlockSpec((tm, tn), lambda i, j: (i, j)),
                   pl.BlockSpec((tm, AB_W), lambda i, j: (i, 0))],
        out_shape=[jax.ShapeDtypeStruct((t, n), BF16),
                   jax.ShapeDtypeStruct((t, AB_W), F32)],
        scratch_shapes=[pltpu.VMEM((tm, d), BF16)],
        compiler_params=_cparams("parallel", "arbitrary"),
        name="inproj",
    )(x2, mod, g, w_main, w_ab)


def _shift_rows(x, prev_row, next_row):
    n = x.shape[0]
    rows = lax.broadcasted_iota(jnp.int32, (n, 1), 0)
    xm = jnp.where(rows == 0, prev_row, pltpu.roll(x, 1, axis=0))
    xp = jnp.where(rows == n - 1, next_row, pltpu.roll(x, n - 1, axis=0))
    return xm, xp


def _conv_kernel(qkv_ref, qp_ref, qn_ref, sc_ref, sp_ref, sn_ref, ab_ref, cq_ref, cs_ref,
                 arow_ref, dtrow_ref, qkvn_ref, kt_ref, yb_ref, gs_ref, aux_ref, *, tiles_per_seq):
    i = pl.program_id(0)
    tb = qkv_ref.shape[0]
    nch = tb // GDN_CHUNK
    hw = GDN_HEADS * GDN_DK
    keep_prev = ((i % tiles_per_seq) != 0).astype(F32)
    keep_next = ((i % tiles_per_seq) != tiles_per_seq - 1).astype(F32)

    x = qkv_ref[...].astype(F32)
    xm, xp = _shift_rows(x, qp_ref[HALO - 1:HALO, :].astype(F32) * keep_prev,
                         qn_ref[0:1, :].astype(F32) * keep_next)
    y = _silu(cq_ref[0:1, :] * xm + cq_ref[1:2, :] * x + cq_ref[2:3, :] * xp)
    for hh in range(2 * GDN_HEADS):
        seg = y[:, hh * GDN_DK:(hh + 1) * GDN_DK]
        nrm = seg * lax.rsqrt(jnp.sum(seg * seg, axis=-1, keepdims=True) + NORM_EPS)
        if hh < GDN_HEADS:
            nrm = nrm * (GDN_DK ** -0.5)
        qkvn_ref[:, hh * GDN_DK:(hh + 1) * GDN_DK] = nrm.astype(BF16)
        if hh >= GDN_HEADS:
            kt = nrm.T
            for c in range(nch):
                kt_ref[c, (hh - GDN_HEADS) * GDN_DK:(hh - GDN_HEADS + 1) * GDN_DK, :] = (
                    kt[:, c * GDN_CHUNK:(c + 1) * GDN_CHUNK].astype(BF16))
    qkvn_ref[:, 2 * hw:] = y[:, 2 * hw:].astype(BF16)

    w = sc_ref.shape[1] // 3
    s = sc_ref[...].astype(F32)
    p = s[:, 2 * w:] * s[:, :w]
    sp = sp_ref[HALO - 1:HALO, :].astype(F32)
    sn = sn_ref[0:1, :].astype(F32)
    pm, pp = _shift_rows(p, sp[:, 2 * w:] * sp[:, :w] * keep_prev, sn[:, 2 * w:] * sn[:, :w] * keep_next)
    yb_ref[...] = (s[:, w:2 * w] * (cs_ref[0:1, :] * pm + cs_ref[1:2, :] * p + cs_ref[2:3, :] * pp)).astype(BF16)

    ab = ab_ref[...]
    g = -arow_ref[...] * _softplus(ab + dtrow_ref[...])
    rows = lax.broadcasted_iota(jnp.int32, (tb, 1), 0) % GDN_CHUNK
    lane = lax.broadcasted_iota(jnp.int32, (1, AB_W), 1)
    pre, suf = g, g
    sh = 1
    while sh < GDN_CHUNK:
        pre = pre + jnp.where(rows >= sh, pltpu.roll(pre, sh, axis=0), 0.0)
        suf = suf + jnp.where(rows < GDN_CHUNK - sh, pltpu.roll(suf, tb - sh, axis=0), 0.0)
        sh *= 2
    tot = pre + suf - g
    gc = jnp.where(lane >= AB_W // 2, suf, pre)
    grp = (lane % (AB_W // 2)) // GDN_HEADS
    gs = jnp.where(grp == 0, gc,
                   jnp.where(grp == 1, jax.nn.sigmoid(ab),
                             jnp.where(grp == 2, jnp.exp(gc),
                                       jnp.where(grp == 3, jnp.exp(tot - gc), jnp.exp(tot)))))
    gs_ref[...] = gs
    gst = gs.T
    for c in range(nch):
        sl = slice(c * GDN_CHUNK, (c + 1) * GDN_CHUNK)
        for d in range(2):
            base = d * (AB_W // 2)
            blk = jnp.zeros((AUX_ROWS, LANES), F32)
            aux_ref[c, d] = blk
            aux_ref[c, d, 0:8, 0:GDN_CHUNK] = gst[base:base + 8, sl]
            aux_ref[c, d, 8:16, 0:GDN_CHUNK] = gst[base + 24:base + 32, sl]
            aux_ref[c, d, 16:24, :] = jnp.broadcast_to(
                gst[base + 32:base + 40, c * GDN_CHUNK:c * GDN_CHUNK + 1], (8, LANES))


def _conv(zin, ab, conv_qkv, conv_sc, arow, dtrow, seq, tb):
    t = zin.shape[0]
    qw = conv_qkv.shape[1]
    sw = conv_sc.shape[1]
    hw = GDN_HEADS * GDN_DK
    assert qw == 3 * hw and zin.shape[1] % qw == 0
    nt = t // tb
    tps = seq // tb
    hb = tb // HALO
    nhalo = t // HALO
    prev = lambda c: (lambda i: (jnp.maximum(i * hb - 1, 0), c))
    nxt = lambda c: (lambda i: (jnp.minimum((i + 1) * hb, nhalo - 1), c))
    nch = tb // GDN_CHUNK
    return pl.pallas_call(
        functools.partial(_conv_kernel, tiles_per_seq=tps),
        grid=(nt,),
        in_specs=[pl.BlockSpec((tb, qw), lambda i: (i, 0)),
                  pl.BlockSpec((HALO, qw), prev(0)),
                  pl.BlockSpec((HALO, qw), nxt(0)),
                  pl.BlockSpec((tb, qw), lambda i: (i, 1)),
                  pl.BlockSpec((HALO, qw), prev(1)),
                  pl.BlockSpec((HALO, qw), nxt(1)),
                  pl.BlockSpec((tb, AB_W), lambda i: (i, 0)),
                  pl.BlockSpec((3, qw), lambda i: (0, 0)),
                  pl.BlockSpec((3, sw), lambda i: (0, 0)),
                  pl.BlockSpec((1, AB_W), lambda i: (0, 0)),
                  pl.BlockSpec((1, AB_W), lambda i: (0, 0))],
        out_specs=[pl.BlockSpec((tb, qw), lambda i: (i, 0)),
                   pl.BlockSpec((nch, hw, GDN_CHUNK), lambda i: (i, 0, 0)),
                   pl.BlockSpec((tb, sw), lambda i: (i, 0)),
                   pl.BlockSpec((tb, AB_W), lambda i: (i, 0)),
                   pl.BlockSpec((nch, 2, AUX_ROWS, LANES), lambda i: (i, 0, 0, 0))],
        out_shape=[jax.ShapeDtypeStruct((t, qw), BF16),
                   jax.ShapeDtypeStruct((t // GDN_CHUNK, hw, GDN_CHUNK), BF16),
                   jax.ShapeDtypeStruct((t, sw), BF16),
                   jax.ShapeDtypeStruct((t, AB_W), F32),
                   jax.ShapeDtypeStruct((t // GDN_CHUNK, 2, AUX_ROWS, LANES), F32)],
        compiler_params=_cparams("parallel"),
        name="conv",
    )(zin, zin, zin, zin, zin, zin, ab, conv_qkv, conv_sc, arow, dtrow)


def _gdn_kernel(qf_ref, kf_ref, vf_ref, ktf_ref, gsf_ref, auxf_ref,
                qb_ref, kb_ref, vb_ref, ktb_ref, gsb_ref, auxb_ref, of_ref, ob_ref, s_scr):
    @pl.when(pl.program_id(1) == 0)
    def _():
        s_scr[...] = jnp.zeros_like(s_scr)

    c = GDN_CHUNK
    nh = GDN_HEADS
    ri = lax.broadcasted_iota(jnp.int32, (c, c), 0)
    ci = lax.broadcasted_iota(jnp.int32, (c, c), 1)
    incl_d = (ri >= ci, ri <= ci)
    strict_d = (ri > ci, ri < ci)
    refs = ((qf_ref, kf_ref, vf_ref, ktf_ref, gsf_ref, auxf_ref, of_ref),
            (qb_ref, kb_ref, vb_ref, ktb_ref, gsb_ref, auxb_ref, ob_ref))
    units = [(d, h) for d in range(2) for h in range(nh)]
    every = lambda f: [f(n, d, h) for n, (d, h) in enumerate(units)]
    hsl = lambda h: slice(h * GDN_DK, (h + 1) * GDN_DK)
    col = lambda d, j: slice(d * (AB_W // 2) + j, d * (AB_W // 2) + j + 1)

    gsv = [gsf_ref[...], gsb_ref[...]]
    q = every(lambda n, d, h: refs[d][0][:, hsl(h)])
    k = every(lambda n, d, h: refs[d][1][:, hsl(h)].astype(F32))
    v = every(lambda n, d, h: refs[d][2][:, hsl(h)].astype(F32))
    kt = every(lambda n, d, h: refs[d][3][0, hsl(h), :])
    gcol = every(lambda n, d, h: gsv[d][:, col(d, h)])
    beta = every(lambda n, d, h: gsv[d][:, col(d, nh + h)])
    egc = every(lambda n, d, h: gsv[d][:, col(d, 2 * nh + h)])
    grow = every(lambda n, d, h: refs[d][5][0, 0, h:h + 1, 0:c])
    kdrow = every(lambda n, d, h: refs[d][5][0, 0, nh + h:nh + h + 1, 0:c])
    etot = every(lambda n, d, h: refs[d][5][0, 0, 2 * nh + h:2 * nh + h + 1, :])

    decay = every(lambda n, d, h: jnp.exp(jnp.where(incl_d[d], gcol[n] - grow[n], NEG_BIG)))
    kb = every(lambda n, d, h: k[n] * beta[n])
    gram = every(lambda n, d, h: _dot(jnp.concatenate([kb[n].astype(BF16), q[n]], axis=0), kt[n]))
    low = every(lambda n, d, h: jnp.where(strict_d[d], gram[n][:c] * decay[n], 0.0))
    attn = every(lambda n, d, h: jnp.where(incl_d[d], gram[n][c:] * decay[n], 0.0).astype(BF16))

    x = every(lambda n, d, h: jnp.concatenate([v[n] * beta[n], kb[n] * egc[n]], axis=-1))
    lp = [l.astype(BF16) for l in low]
    x = every(lambda n, d, h: x[n] - _dot(lp[n], x[n].astype(BF16)))
    p = 2
    while p < c:
        lp = [_dot(l, l).astype(BF16) for l in lp]
        x = every(lambda n, d, h: x[n] + _dot(lp[n], x[n].astype(BF16)))
        p *= 2

    s = [s_scr[n] for n in range(len(units))]
    sb = [t.astype(BF16) for t in s]
    qg = every(lambda n, d, h: (q[n].astype(F32) * egc[n]).astype(BF16))
    ws = every(lambda n, d, h: _dot(jnp.concatenate([x[n][:, GDN_DV:].astype(BF16), qg[n]], axis=0), sb[n]))
    vnb = every(lambda n, d, h: (x[n][:, :GDN_DV] - ws[n][:c]).astype(BF16))
    o = every(lambda n, d, h: ws[n][c:] + _dot(attn[n], vnb[n]))
    for n, (d, h) in enumerate(units):
        refs[d][6][:, hsl(h)] = o[n].astype(BF16)
    ktd = every(lambda n, d, h: (kt[n].astype(F32) * kdrow[n]).astype(BF16))
    for n in range(len(units)):
        s_scr[n] = s[n] * etot[n] + _dot(ktd[n], vnb[n])


def _gdn(qkvn, kt, gs, aux, batch, seq):
    t = qkvn.shape[0]
    hw = GDN_HEADS * GDN_DK
    c = GDN_CHUNK
    nc = seq // c
    fwd = lambda b, j: b * nc + j
    bwd = lambda b, j: b * nc + nc - 1 - j

    def specs(ch, d):
        return [pl.BlockSpec((c, hw), lambda b, j: (ch(b, j), 0)),
                pl.BlockSpec((c, hw), lambda b, j: (ch(b, j), 1)),
                pl.BlockSpec((c, hw), lambda b, j: (ch(b, j), 2)),
                pl.BlockSpec((1, hw, c), lambda b, j: (ch(b, j), 0, 0)),
                pl.BlockSpec((c, AB_W), lambda b, j: (ch(b, j), 0)),
                pl.BlockSpec((1, 1, AUX_ROWS, LANES), lambda b, j: (ch(b, j), d, 0, 0))]

    return pl.pallas_call(
        _gdn_kernel,
        grid=(batch, nc),
        in_specs=specs(fwd, 0) + specs(bwd, 1),
        out_specs=[pl.BlockSpec((c, hw), lambda b, j: (fwd(b, j), 0)),
                   pl.BlockSpec((c, hw), lambda b, j: (bwd(b, j), 0))],
        out_shape=[jax.ShapeDtypeStruct((t, hw), BF16), jax.ShapeDtypeStruct((t, hw), BF16)],
        scratch_shapes=[pltpu.VMEM((2 * GDN_HEADS, GDN_DK, GDN_DV), F32)],
        compiler_params=_cparams("parallel", "arbitrary"),
        name="gdn",
    )(qkvn, qkvn, qkvn, kt, gs, aux, qkvn, qkvn, qkvn, kt, gs, aux)


def _mix_kernel(of_ref, ob_ref, z_ref, mg_ref, yb_ref, x_ref, mod_ref, onorm_ref, post1_ref, pre2_ref,
                wa_ref, wb_ref, wo_ref, x1_ref, h2t_ref, gated_scr):
    o = of_ref[...].astype(F32) + ob_ref[...].astype(F32)
    z = z_ref[...].astype(F32)
    for h in range(GDN_HEADS):
        hs = slice(h * GDN_DV, (h + 1) * GDN_DV)
        gated_scr[:, hs] = (_rms(o[:, hs], onorm_ref[...]) * _silu(z[:, hs])).astype(BF16)
    ya = _dot(gated_scr[...], wa_ref[...])
    yb = _dot(yb_ref[...], wb_ref[...])
    d = ya.shape[1]
    mg = mg_ref[...].astype(F32)
    mix = jax.nn.sigmoid(mg[:, :d]) * ya + jax.nn.sigmoid(mg[:, d:]) * yb
    mo = _dot(mix.astype(BF16), wo_ref[...])
    x1 = x_ref[...] + mod_ref[0, 2:3, :] * _rms(mo, post1_ref[...])
    x1_ref[...] = x1
    h2 = _rms(x1, pre2_ref[...]) * (1.0 + mod_ref[0, 4:5, :]) + mod_ref[0, 3:4, :]
    h2t_ref[...] = h2.T.astype(BF16)


def _mix(o_f, o_b, zin, yb, x2, mod, onorm, post1, pre2, wa, wb, wo, seq, tm):
    t, d = x2.shape
    hw = o_f.shape[1]
    tps = seq // tm
    zcol = zin.shape[1] // hw - 1
    mgcol = (zin.shape[1] - hw) // (2 * d) - 1
    full = lambda shape: pl.BlockSpec(shape, lambda i: (0,) * len(shape))
    return pl.pallas_call(
        _mix_kernel,
        grid=(t // tm,),
        in_specs=[pl.BlockSpec((tm, hw), lambda i: (i, 0)),
                  pl.BlockSpec((tm, hw), lambda i: (i, 0)),
                  pl.BlockSpec((tm, hw), lambda i: (i, zcol)),
                  pl.BlockSpec((tm, 2 * d), lambda i: (i, mgcol)),
                  pl.BlockSpec((tm, yb.shape[1]), lambda i: (i, 0)),
                  pl.BlockSpec((tm, d), lambda i: (i, 0)),
                  pl.BlockSpec((1, MOD_ROWS, d), lambda i: (i // tps, 0, 0)),
                  full((1, GDN_DV)), full((1, d)), full((1, d)),
                  full(wa.shape), full(wb.shape), full(wo.shape)],
        out_specs=[pl.BlockSpec((tm, d), lambda i: (i, 0)),
                   pl.BlockSpec((d, tm), lambda i: (0, i))],
        out_shape=[jax.ShapeDtypeStruct((t, d), F32),
                   jax.ShapeDtypeStruct((d, t), BF16)],
        scratch_shapes=[pltpu.VMEM((tm, hw), BF16)],
        compiler_params=_cparams("parallel"),
        name="mix",
    )(o_f, o_b, zin, zin, yb, x2, mod, onorm, post1, pre2, wa, wb, wo)


def _top_vals(s, k):
    vals = []
    cur = s
    for _ in range(k):
        m = jnp.max(cur, axis=0, keepdims=True)
        vals.append(m)
        cur = jnp.where(cur == m, -jnp.inf, cur)
    return vals


def _peerq_kernel(h2t_ref, wq_ref, keys_ref, a_ref, b_ref, thr_ref):
    qt = _dot(wq_ref[...], h2t_ref[...]).astype(BF16)
    dq = keys_ref.shape[2]
    kk = PEER_TOPK
    a_sorted, b_sorted, b_full = [], [], []
    for h in range(PEER_HEADS):
        s0 = _dot(keys_ref[2 * h], qt[(2 * h) * dq:(2 * h + 1) * dq, :])
        s1 = _dot(keys_ref[2 * h + 1], qt[(2 * h + 1) * dq:(2 * h + 2) * dq, :])
        v0 = _top_vals(s0, kk)
        v1 = _top_vals(s1, kk)
        a_ref[h] = jnp.exp(s0 - v0[0]).astype(BF16).astype(F32)
        b_full.append(jnp.exp(s1 - v1[0]))
        a_sorted.append([jnp.exp(v - v0[0]) for v in v0])
        b_sorted.append([jnp.exp(v - v1[0]) for v in v1])
    a_k = [jnp.concatenate([a_sorted[h][r] for h in range(PEER_HEADS)], axis=0) for r in range(kk)]
    b_k = [jnp.concatenate([b_sorted[h][r] for h in range(PEER_HEADS)], axis=0) for r in range(kk)]
    pairs = [(r0, r1) for r0 in range(kk) for r1 in range(kk // (r0 + 1))]
    cand = [a_k[r0] * b_k[r1] for r0, r1 in pairs]
    cur = list(cand)
    z = jnp.zeros_like(cand[0])
    m = z
    for _ in range(kk):
        m = jnp.maximum(functools.reduce(jnp.maximum, cur), 0.0)
        z = z + m
        cur = [jnp.where(x == m, -1.0, x) for x in cur]
    inv_z = 1.0 / z
    a_b = [a.astype(BF16) for a in a_k]
    b_b = [(b * inv_z).astype(BF16) for b in b_k]
    thr = functools.reduce(jnp.minimum, [
        jnp.where(cand[n] >= m, (a_b[r0] * b_b[r1]).astype(F32), jnp.inf) for n, (r0, r1) in enumerate(pairs)])
    thr_ref[...] = thr
    for h in range(PEER_HEADS):
        b_ref[h] = (b_full[h] * inv_z[h:h + 1, :]).astype(BF16)


def _peerq(h2t, wq_t, keys, tm):
    d, t = h2t.shape
    nk = keys.shape[1]
    full = lambda shape: pl.BlockSpec(shape, lambda i: (0,) * len(shape))
    return pl.pallas_call(
        _peerq_kernel,
        grid=(t // tm,),
        in_specs=[pl.BlockSpec((d, tm), lambda i: (0, i)), full(wq_t.shape), full(keys.shape)],
        out_specs=[pl.BlockSpec((PEER_HEADS, nk, tm), lambda i: (0, 0, i)),
                   pl.BlockSpec((PEER_HEADS, nk, tm), lambda i: (0, 0, i)),
                   pl.BlockSpec((PEER_HEADS, tm), lambda i: (0, i))],
        out_shape=[jax.ShapeDtypeStruct((PEER_HEADS, nk, t), F32),
                   jax.ShapeDtypeStruct((PEER_HEADS, nk, t), BF16),
                   jax.ShapeDtypeStruct((PEER_HEADS, t), F32)],
        compiler_params=_cparams("parallel"),
        name="peerq",
    )(h2t, wq_t, keys)


def _gelu(x):
    return 0.5 * x * (1.0 + lax.erf(x * (2.0 ** -0.5)))


def _peer_kernel(h2t_ref, u_ref, vt_ref, a_ref, b_ref, thr_ref, x1_ref, mod_ref, post2_ref,
                 out_ref, acc_ref, act_ref, w2_ref, *, ne, total):
    s = pl.program_id(0)
    nk = a_ref.shape[1]
    te, tm = act_ref.shape[1:]
    rows = te // nk
    d = acc_ref.shape[0]
    pack = 2 * 8
    dpieces = d // (rows * nk)

    @pl.when(s == 0)
    def _():
        act_ref[...] = jnp.zeros_like(act_ref)
        w2_ref[...] = jnp.zeros_like(w2_ref)

    slot = s % 2
    iv = jnp.clip(s - 1, 0, total - 1) % ne
    first = (jnp.maximum(s - 2, 0) % ne) == 0
    h2t = h2t_ref[...]
    thr = thr_ref[...].astype(BF16)
    thr_h = [jnp.broadcast_to(thr[h:h + 1, :], (pack, tm)) for h in range(PEER_HEADS)]
    w2_prev = w2_ref[1 - slot]
    for r in range(rows):
        rs = slice(r * nk, (r + 1) * nk)
        act_ref[slot, rs, :] = _dot(u_ref[rs, :], h2t)
        a16 = [jnp.broadcast_to(a_ref[h, pl.ds(iv * rows + r, 1), :], (pack, tm)).astype(BF16)
               for h in range(PEER_HEADS)]
        gel = _gelu(act_ref[1 - slot, rs, :]).astype(BF16)
        for g in range(nk // pack):
            gs = slice(g * pack, (g + 1) * pack)
            wsum = None
            for h in range(PEER_HEADS):
                p = a16[h] * b_ref[h, gs, :]
                term = jnp.where(p >= thr_h[h], p, jnp.zeros_like(p))
                wsum = term if wsum is None else wsum + term
            w2_ref[slot, r * nk + g * pack:r * nk + (g + 1) * pack, :] = gel[gs, :] * wsum
        for q in range(dpieces):
            ds = slice((r * dpieces + q) * nk, (r * dpieces + q + 1) * nk)
            part = _dot(vt_ref[ds, :], w2_prev)
            acc_ref[ds, :] = part + jnp.where(first, 0.0, acc_ref[ds, :])

    @pl.when((s >= 2) & ((s - 2) % ne == ne - 1))
    def _():
        f = acc_ref[...].T
        out_ref[...] = x1_ref[...] + mod_ref[0, 5:6, :] * _rms(f, post2_ref[...])


def _peer(h2t, u, vt, a, b, thr, x1, mod, post2, seq, tm, te):
    d, t = h2t.shape
    nk = a.shape[1]
    ne = u.shape[0] // te
    tps = seq // tm
    total = (t // tm) * ne
    assert te % nk == 0 and d % te == 0
    item = lambda s, lag: jnp.clip(s - lag, 0, total - 1)
    return pl.pallas_call(
        functools.partial(_peer_kernel, ne=ne, total=total),
        grid=(total + 2,),
        in_specs=[pl.BlockSpec((d, tm), lambda s: (0, item(s, 0) // ne)),
                  pl.BlockSpec((te, d), lambda s: (item(s, 0) % ne, 0)),
                  pl.BlockSpec((d, te), lambda s: (0, item(s, 2) % ne)),
                  pl.BlockSpec((PEER_HEADS, nk, tm), lambda s: (0, 0, item(s, 1) // ne)),
                  pl.BlockSpec((PEER_HEADS, nk, tm), lambda s: (0, 0, item(s, 1) // ne)),
                  pl.BlockSpec((PEER_HEADS, tm), lambda s: (0, item(s, 1) // ne)),
                  pl.BlockSpec((tm, d), lambda s: (item(s, 2) // ne, 0)),
                  pl.BlockSpec((1, MOD_ROWS, d), lambda s: (item(s, 2) // ne // tps, 0, 0)),
                  pl.BlockSpec((1, d), lambda s: (0, 0))],
        out_specs=pl.BlockSpec((tm, d), lambda s: (item(s, 2) // ne, 0)),
        out_shape=jax.ShapeDtypeStruct((t, d), F32),
        scratch_shapes=[pltpu.VMEM((d, tm), F32), pltpu.VMEM((2, te, tm), F32), pltpu.VMEM((2, te, tm), BF16)],
        compiler_params=_cparams("arbitrary"),
        name="peer",
    )(h2t, u, vt, a, b, thr, x1, mod, post2)


def _tile(n, pref):
    return pref if n % pref == 0 else n


def _layer(x, mod, p):
    bsz, seq, d = x.shape
    t = bsz * seq
    x2 = x.reshape(t, d)
    tm = _tile(seq, 512)
    zin, ab = _inproj(x2, mod, p["pre1"], p["w_main"], p["w_ab"], seq, _tile(seq, 1024))
    qkvn, kt, yb, gs, aux = _conv(zin, ab, p["conv_qkv"], p["conv_sc"], p["arow"], p["dtrow"], seq, _tile(seq, 256))
    o_f, o_b = _gdn(qkvn, kt, gs, aux, bsz, seq)
    x1, h2t = _mix(o_f, o_b, zin, yb, x2, mod, p["onorm"], p["post1"], p["pre2"], p["wa"], p["wb"], p["wo"], seq, tm)
    a, b, thr = _peerq(h2t, p["wq_t"], p["keys"], _tile(seq, 256))
    out = _peer(h2t, p["u"], p["vt"], a, b, thr, x1, mod, p["post2"], seq, tm, 512)
    return out.reshape(bsz, seq, d)


def _lane_rows(vals_f, vals_b):
    def half(v):
        return jnp.concatenate([jnp.tile(v, 5), jnp.zeros((AB_W // 2 - 5 * GDN_HEADS,), F32)])
    return jnp.concatenate([half(vals_f), half(vals_b)]).reshape(1, AB_W)


def _prep(l, pre_norm1, post_norm1, pre_norm2, post_norm2, w_in, conv_qkv, a_log, dt_bias, out_norm,
          w_branch_a, conv_sc, w_branch_b, w_out, w_peer_q, peer_sub_keys, peer_u, peer_v):
    d = w_in.shape[1]
    hw = GDN_HEADS * GDN_DK
    qkv_w = 2 * hw + GDN_HEADS * GDN_DV
    z_w = GDN_HEADS * GDN_DV
    nh2 = 2 * GDN_HEADS
    sc_w = 3 * conv_sc.shape[2]
    o_z, o_b, o_a, o_sc = qkv_w, qkv_w + z_w, qkv_w + z_w + nh2, qkv_w + z_w + 2 * nh2
    o_mg = o_sc + sc_w
    w = w_in[l]
    w_main = jnp.concatenate([w[:, :qkv_w], w[:, o_sc:o_mg], w[:, o_mg:], w[:, o_z:o_b]], axis=1).astype(BF16)
    beta = w[:, o_b:o_a]
    alpha = w[:, o_a:o_sc]
    pad = jnp.zeros((d, AB_W // 2 - 5 * GDN_HEADS), F32)

    def half(i):
        al = alpha[:, i * GDN_HEADS:(i + 1) * GDN_HEADS]
        return jnp.concatenate([al, beta[:, i * GDN_HEADS:(i + 1) * GDN_HEADS], al, al, al, pad], axis=1)

    w_ab = jnp.concatenate([half(0), half(1)], axis=1).astype(BF16)
    ea = jnp.exp(a_log[l].astype(F32))
    nkeys, dq = peer_sub_keys.shape[3], peer_sub_keys.shape[4]
    return dict(
        pre1=pre_norm1[l].reshape(1, d), post1=post_norm1[l].reshape(1, d),
        pre2=pre_norm2[l].reshape(1, d), post2=post_norm2[l].reshape(1, d),
        w_main=w_main, w_ab=w_ab, conv_qkv=conv_qkv[l], conv_sc=conv_sc[l],
        arow=_lane_rows(ea[0], ea[1]), dtrow=_lane_rows(dt_bias[l, 0], dt_bias[l, 1]),
        onorm=out_norm[l].reshape(1, GDN_DV),
        wa=w_branch_a[l].astype(BF16), wb=w_branch_b[l].astype(BF16), wo=w_out[l].astype(BF16),
        wq_t=w_peer_q[l].T.astype(BF16),
        keys=peer_sub_keys[l].reshape(PEER_HEADS * 2, nkeys, dq).astype(BF16),
        u=peer_u[l].astype(BF16), vt=peer_v[l].T.astype(BF16))


def kernel(x_prompt, x_sample, c_prompt, c_sample, pre_norm1, post_norm1, pre_norm2, post_norm2, w_ada, b_ada, w_in, conv_qkv, a_log, dt_bias, out_norm, w_branch_a, conv_sc, w_branch_b, w_out, w_peer_q, peer_sub_keys, peer_u, peer_v):
    depth = w_in.shape[0]
    d = x_prompt.shape[2]
    nb = x_prompt.shape[0]
    xs = [x_prompt, x_sample]
    c_all = jnp.concatenate([c_prompt, c_sample], axis=0)
    for l in range(depth):
        p = _prep(l, pre_norm1, post_norm1, pre_norm2, post_norm2, w_in, conv_qkv, a_log, dt_bias, out_norm,
                  w_branch_a, conv_sc, w_branch_b, w_out, w_peer_q, peer_sub_keys, peer_u, peer_v)
        mod = _ada(c_all, w_ada[l], b_ada[l]).reshape(c_all.shape[0], N_MOD, d)
        mod = jnp.pad(mod, ((0, 0), (0, MOD_ROWS - N_MOD), (0, 0)))
        xs = [_layer(xs[0], mod[:nb], p), _layer(xs[1], mod[nb:], p)]
    return (xs[0], xs[1])
```

```python
import functools
import math

import jax
import jax.numpy as jnp
from jax import lax
from jax.experimental import pallas as pl
from jax.experimental.pallas import tpu as pltpu

F32 = jnp.float32
BF16 = jnp.bfloat16
GATE_BITS = jnp.uint32

NORM_EPS = 1e-6
N_MOD = 6
MOD_ROWS = 8
GDN_HEADS = 8
GDN_DK = 128
GDN_DV = 128
GDN_CHUNK = 64
PEER_HEADS = 8
PEER_NKEYS = 128
PEER_TOPK = 16
LANES = 128
HALO = 16
AB_W = 128
AUX_ROWS = 24
TOKEN_BLOCK = 256
NEG_BIG = -1e30
VMEM_LIMIT = 56 * 1024 * 1024


def _silu(x):
    return x * jax.nn.sigmoid(x)


def _softplus(x):
    return jnp.maximum(x, 0.0) + jnp.log1p(jnp.exp(-jnp.abs(x)))


def _rms(x, g):
    return x * lax.rsqrt(jnp.mean(x * x, axis=-1, keepdims=True) + NORM_EPS) * g


def _dot(a, b):
    return jnp.dot(a, b, preferred_element_type=F32)


def _cparams(*sem):
    return pltpu.CompilerParams(dimension_semantics=sem, vmem_limit_bytes=VMEM_LIMIT)


def _ada_kernel(c_ref, w_ref, b_ref, o_ref):
    c = c_ref[...]
    o_ref[...] = _dot(_silu(c).astype(BF16), w_ref[...].astype(BF16)) + b_ref[...]


def _ada(c, w, b, tn=1536):
    m, d = c.shape
    n = w.shape[1]
    return pl.pallas_call(
        _ada_kernel,
        grid=(n // tn,),
        in_specs=[pl.BlockSpec((m, d), lambda j: (0, 0)),
                  pl.BlockSpec((d, tn), lambda j: (0, j)),
                  pl.BlockSpec((1, tn), lambda j: (0, j))],
        out_specs=pl.BlockSpec((m, tn), lambda j: (0, j)),
        out_shape=jax.ShapeDtypeStruct((m, n), F32),
        compiler_params=_cparams("parallel"),
        name="ada",
    )(c, w, b.reshape(1, n))


def _inproj_kernel(x_ref, mod_ref, g_ref, w_ref, wab_ref, zin_ref, ab_ref, *, tn):
    y = _rms(x_ref[...], g_ref[...])
    h = y * (1.0 + mod_ref[0, 1:2, :]) + mod_ref[0, 0:1, :]
    hb = h.astype(BF16)
    ab_ref[...] = _dot(hb, wab_ref[...])
    for j in range(w_ref.shape[1] // tn):
        zin_ref[:, j * tn:(j + 1) * tn] = _dot(hb, w_ref[:, j * tn:(j + 1) * tn]).astype(BF16)


def _inproj(x2, mod, g, w_main, w_ab, seq, tm, tn=1024):
    t, d = x2.shape
    n = w_main.shape[1]
    tps = seq // tm
    assert n % tn == 0
    return pl.pallas_call(
        functools.partial(_inproj_kernel, tn=tn),
        grid=(t // tm,),
        in_specs=[pl.BlockSpec((tm, d), lambda i: (i, 0)),
                  pl.BlockSpec((1, MOD_ROWS, d), lambda i: (i // tps, 0, 0)),
                  pl.BlockSpec((1, d), lambda i: (0, 0)),
                  pl.BlockSpec((d, n), lambda i: (0, 0), pipeline_mode=pl.Buffered(1)),
                  pl.BlockSpec((d, AB_W), lambda i: (0, 0))],
        out_specs=[pl.BlockSpec((tm, n), lambda i: (i, 0)),
                   pl.BlockSpec((tm, AB_W), lambda i: (i, 0))],
        out_shape=[jax.ShapeDtypeStruct((t, n), BF16),
                   jax.ShapeDtypeStruct((t, AB_W), F32)],
        compiler_params=_cparams("parallel"),
        name="inproj",
    )(x2, mod, g, w_main, w_ab)


def _shift_rows(x, prev_row, next_row):
    n = x.shape[0]
    rows = lax.broadcasted_iota(jnp.int32, (n, 1), 0)
    xm = jnp.where(rows == 0, prev_row, pltpu.roll(x, 1, axis=0))
    xp = jnp.where(rows == n - 1, next_row, pltpu.roll(x, n - 1, axis=0))
    return xm, xp


def _conv_kernel(qkv_ref, qp_ref, qn_ref, sc_ref, sp_ref, sn_ref, ab_ref, cq_ref, cs_ref,
                 arow_ref, dtrow_ref, qkvn_ref, kt_ref, yb_ref, gs_ref, aux_ref, *, tiles_per_seq):
    i = pl.program_id(0)
    tb = qkv_ref.shape[0]
    nch = tb // GDN_CHUNK
    hw = GDN_HEADS * GDN_DK
    keep_prev = ((i % tiles_per_seq) != 0).astype(F32)
    keep_next = ((i % tiles_per_seq) != tiles_per_seq - 1).astype(F32)

    x = qkv_ref[...].astype(F32)
    xm, xp = _shift_rows(x, qp_ref[HALO - 1:HALO, :].astype(F32) * keep_prev,
                         qn_ref[0:1, :].astype(F32) * keep_next)
    y = _silu(cq_ref[0:1, :] * xm + cq_ref[1:2, :] * x + cq_ref[2:3, :] * xp)
    for hh in range(2 * GDN_HEADS):
        seg = y[:, hh * GDN_DK:(hh + 1) * GDN_DK]
        nrm = seg * lax.rsqrt(jnp.sum(seg * seg, axis=-1, keepdims=True) + NORM_EPS)
        if hh < GDN_HEADS:
            nrm = nrm * (GDN_DK ** -0.5)
        qkvn_ref[:, hh * GDN_DK:(hh + 1) * GDN_DK] = nrm.astype(BF16)
        if hh >= GDN_HEADS:
            kt = nrm.T
            for c in range(nch):
                kt_ref[c, (hh - GDN_HEADS) * GDN_DK:(hh - GDN_HEADS + 1) * GDN_DK, :] = (
                    kt[:, c * GDN_CHUNK:(c + 1) * GDN_CHUNK].astype(BF16))
    qkvn_ref[:, 2 * hw:] = y[:, 2 * hw:].astype(BF16)

    w = sc_ref.shape[1] // 3
    s = sc_ref[...].astype(F32)
    p = s[:, 2 * w:] * s[:, :w]
    sp = sp_ref[HALO - 1:HALO, :].astype(F32)
    sn = sn_ref[0:1, :].astype(F32)
    pm, pp = _shift_rows(p, sp[:, 2 * w:] * sp[:, :w] * keep_prev, sn[:, 2 * w:] * sn[:, :w] * keep_next)
    yb_ref[...] = (s[:, w:2 * w] * (cs_ref[0:1, :] * pm + cs_ref[1:2, :] * p + cs_ref[2:3, :] * pp)).astype(BF16)

    ab = ab_ref[...]
    g = -arow_ref[...] * _softplus(ab + dtrow_ref[...])
    rows = lax.broadcasted_iota(jnp.int32, (tb, 1), 0) % GDN_CHUNK
    lane = lax.broadcasted_iota(jnp.int32, (1, AB_W), 1)
    pre, suf = g, g
    sh = 1
    while sh < GDN_CHUNK:
        pre = pre + jnp.where(rows >= sh, pltpu.roll(pre, sh, axis=0), 0.0)
        suf = suf + jnp.where(rows < GDN_CHUNK - sh, pltpu.roll(suf, tb - sh, axis=0), 0.0)
        sh *= 2
    tot = pre + suf - g
    gc = jnp.where(lane >= AB_W // 2, suf, pre)
    grp = (lane % (AB_W // 2)) // GDN_HEADS
    gs = jnp.where(grp == 0, gc,
                   jnp.where(grp == 1, jax.nn.sigmoid(ab),
                             jnp.where(grp == 2, jnp.exp(gc),
                                       jnp.where(grp == 3, jnp.exp(tot - gc), jnp.exp(tot)))))
    gs_ref[...] = gs
    gst = gs.T
    for c in range(nch):
        sl = slice(c * GDN_CHUNK, (c + 1) * GDN_CHUNK)
        for d in range(2):
            base = d * (AB_W // 2)
            blk = jnp.zeros((AUX_ROWS, LANES), F32)
            aux_ref[c, d] = blk
            aux_ref[c, d, 0:8, 0:GDN_CHUNK] = gst[base:base + 8, sl]
            aux_ref[c, d, 8:16, 0:GDN_CHUNK] = gst[base + 24:base + 32, sl]
            aux_ref[c, d, 16:24, :] = jnp.broadcast_to(
                gst[base + 32:base + 40, c * GDN_CHUNK:c * GDN_CHUNK + 1], (8, LANES))


def _conv(zin, ab, conv_qkv, conv_sc, arow, dtrow, seq, tb):
    t = zin.shape[0]
    qw = conv_qkv.shape[1]
    sw = conv_sc.shape[1]
    hw = GDN_HEADS * GDN_DK
    assert qw == 3 * hw and zin.shape[1] % qw == 0
    nt = t // tb
    tps = seq // tb
    hb = tb // HALO
    nhalo = t // HALO
    prev = lambda c: (lambda i: (jnp.maximum(i * hb - 1, 0), c))
    nxt = lambda c: (lambda i: (jnp.minimum((i + 1) * hb, nhalo - 1), c))
    nch = tb // GDN_CHUNK
    return pl.pallas_call(
        functools.partial(_conv_kernel, tiles_per_seq=tps),
        grid=(nt,),
        in_specs=[pl.BlockSpec((tb, qw), lambda i: (i, 0)),
                  pl.BlockSpec((HALO, qw), prev(0)),
                  pl.BlockSpec((HALO, qw), nxt(0)),
                  pl.BlockSpec((tb, qw), lambda i: (i, 1)),
                  pl.BlockSpec((HALO, qw), prev(1)),
                  pl.BlockSpec((HALO, qw), nxt(1)),
                  pl.BlockSpec((tb, AB_W), lambda i: (i, 0)),
                  pl.BlockSpec((3, qw), lambda i: (0, 0)),
                  pl.BlockSpec((3, sw), lambda i: (0, 0)),
                  pl.BlockSpec((1, AB_W), lambda i: (0, 0)),
                  pl.BlockSpec((1, AB_W), lambda i: (0, 0))],
        out_specs=[pl.BlockSpec((tb, qw), lambda i: (i, 0)),
                   pl.BlockSpec((nch, hw, GDN_CHUNK), lambda i: (i, 0, 0)),
                   pl.BlockSpec((tb, sw), lambda i: (i, 0)),
                   pl.BlockSpec((tb, AB_W), lambda i: (i, 0)),
                   pl.BlockSpec((nch, 2, AUX_ROWS, LANES), lambda i: (i, 0, 0, 0))],
        out_shape=[jax.ShapeDtypeStruct((t, qw), BF16),
                   jax.ShapeDtypeStruct((t // GDN_CHUNK, hw, GDN_CHUNK), BF16),
                   jax.ShapeDtypeStruct((t, sw), BF16),
                   jax.ShapeDtypeStruct((t, AB_W), F32),
                   jax.ShapeDtypeStruct((t // GDN_CHUNK, 2, AUX_ROWS, LANES), F32)],
        compiler_params=_cparams("parallel"),
        name="conv",
    )(zin, zin, zin, zin, zin, zin, ab, conv_qkv, conv_sc, arow, dtrow)


def _gdn_kernel(qf_ref, kf_ref, vf_ref, ktf_ref, gsf_ref, auxf_ref,
                qb_ref, kb_ref, vb_ref, ktb_ref, gsb_ref, auxb_ref, of_ref, ob_ref, s_scr):
    @pl.when(pl.program_id(1) == 0)
    def _():
        s_scr[...] = jnp.zeros_like(s_scr)

    c = GDN_CHUNK
    nh = GDN_HEADS
    ri = lax.broadcasted_iota(jnp.int32, (c, c), 0)
    ci = lax.broadcasted_iota(jnp.int32, (c, c), 1)
    incl_d = (ri >= ci, ri <= ci)
    strict_d = (ri > ci, ri < ci)
    refs = ((qf_ref, kf_ref, vf_ref, ktf_ref, gsf_ref, auxf_ref, of_ref),
            (qb_ref, kb_ref, vb_ref, ktb_ref, gsb_ref, auxb_ref, ob_ref))
    units = [(d, h) for d in range(2) for h in range(nh)]
    every = lambda f: [f(n, d, h) for n, (d, h) in enumerate(units)]
    hsl = lambda h: slice(h * GDN_DK, (h + 1) * GDN_DK)
    col = lambda d, j: slice(d * (AB_W // 2) + j, d * (AB_W // 2) + j + 1)

    gsv = [gsf_ref[...], gsb_ref[...]]
    q = every(lambda n, d, h: refs[d][0][:, hsl(h)])
    k = every(lambda n, d, h: refs[d][1][:, hsl(h)].astype(F32))
    v = every(lambda n, d, h: refs[d][2][:, hsl(h)].astype(F32))
    kt = every(lambda n, d, h: refs[d][3][0, hsl(h), :])
    gcol = every(lambda n, d, h: gsv[d][:, col(d, h)])
    beta = every(lambda n, d, h: gsv[d][:, col(d, nh + h)])
    egc = every(lambda n, d, h: gsv[d][:, col(d, 2 * nh + h)])
    grow = every(lambda n, d, h: refs[d][5][0, 0, h:h + 1, 0:c])
    kdrow = every(lambda n, d, h: refs[d][5][0, 0, nh + h:nh + h + 1, 0:c])
    etot = every(lambda n, d, h: refs[d][5][0, 0, 2 * nh + h:2 * nh + h + 1, :])

    decay = every(lambda n, d, h: jnp.exp(jnp.where(incl_d[d], gcol[n] - grow[n], NEG_BIG)))
    kb = every(lambda n, d, h: k[n] * beta[n])
    gram = every(lambda n, d, h: _dot(jnp.concatenate([kb[n].astype(BF16), q[n]], axis=0), kt[n]))
    low = every(lambda n, d, h: jnp.where(strict_d[d], gram[n][:c] * decay[n], 0.0))
    attn = every(lambda n, d, h: jnp.where(incl_d[d], gram[n][c:] * decay[n], 0.0).astype(BF16))

    x = every(lambda n, d, h: jnp.concatenate([v[n] * beta[n], kb[n] * egc[n]], axis=-1))
    lp = [l.astype(BF16) for l in low]
    x = every(lambda n, d, h: x[n] - _dot(lp[n], x[n].astype(BF16)))
    p = 2
    while p < c:
        lp = [_dot(l, l).astype(BF16) for l in lp]
        x = every(lambda n, d, h: x[n] + _dot(lp[n], x[n].astype(BF16)))
        p *= 2

    s = [s_scr[n] for n in range(len(units))]
    sb = [t.astype(BF16) for t in s]
    qg = every(lambda n, d, h: (q[n].astype(F32) * egc[n]).astype(BF16))
    ws = every(lambda n, d, h: _dot(jnp.concatenate([x[n][:, GDN_DV:].astype(BF16), qg[n]], axis=0), sb[n]))
    vnb = every(lambda n, d, h: (x[n][:, :GDN_DV] - ws[n][:c]).astype(BF16))
    o = every(lambda n, d, h: ws[n][c:] + _dot(attn[n], vnb[n]))
    for n, (d, h) in enumerate(units):
        refs[d][6][:, hsl(h)] = o[n].astype(BF16)
    ktd = every(lambda n, d, h: (kt[n].astype(F32) * kdrow[n]).astype(BF16))
    for n in range(len(units)):
        s_scr[n] = s[n] * etot[n] + _dot(ktd[n], vnb[n])


def _gdn(qkvn, kt, gs, aux, batch, seq):
    t = qkvn.shape[0]
    hw = GDN_HEADS * GDN_DK
    c = GDN_CHUNK
    nc = seq // c
    fwd = lambda b, j: b * nc + j
    bwd = lambda b, j: b * nc + nc - 1 - j

    def specs(ch, d):
        return [pl.BlockSpec((c, hw), lambda b, j: (ch(b, j), 0)),
                pl.BlockSpec((c, hw), lambda b, j: (ch(b, j), 1)),
                pl.BlockSpec((c, hw), lambda b, j: (ch(b, j), 2)),
                pl.BlockSpec((1, hw, c), lambda b, j: (ch(b, j), 0, 0)),
                pl.BlockSpec((c, AB_W), lambda b, j: (ch(b, j), 0)),
                pl.BlockSpec((1, 1, AUX_ROWS, LANES), lambda b, j: (ch(b, j), d, 0, 0))]

    return pl.pallas_call(
        _gdn_kernel,
        grid=(batch, nc),
        in_specs=specs(fwd, 0) + specs(bwd, 1),
        out_specs=[pl.BlockSpec((c, hw), lambda b, j: (fwd(b, j), 0)),
                   pl.BlockSpec((c, hw), lambda b, j: (bwd(b, j), 0))],
        out_shape=[jax.ShapeDtypeStruct((t, hw), BF16), jax.ShapeDtypeStruct((t, hw), BF16)],
        scratch_shapes=[pltpu.VMEM((2 * GDN_HEADS, GDN_DK, GDN_DV), F32)],
        compiler_params=_cparams("parallel", "arbitrary"),
        name="gdn",
    )(qkvn, qkvn, qkvn, kt, gs, aux, qkvn, qkvn, qkvn, kt, gs, aux)


def _mix_kernel(of_ref, ob_ref, z_ref, mg_ref, yb_ref, x_ref, mod_ref, onorm_ref, post1_ref, pre2_ref,
                wa_ref, wb_ref, wo_ref, x1_ref, h2t_ref, gated_scr):
    o = of_ref[...].astype(F32) + ob_ref[...].astype(F32)
    z = z_ref[...].astype(F32)
    for h in range(GDN_HEADS):
        hs = slice(h * GDN_DV, (h + 1) * GDN_DV)
        gated_scr[:, hs] = (_rms(o[:, hs], onorm_ref[...]) * _silu(z[:, hs])).astype(BF16)
    ya = _dot(gated_scr[...], wa_ref[...])
    yb = _dot(yb_ref[...], wb_ref[...])
    d = ya.shape[1]
    mg = mg_ref[...].astype(F32)
    mix = jax.nn.sigmoid(mg[:, :d]) * ya + jax.nn.sigmoid(mg[:, d:]) * yb
    mo = _dot(mix.astype(BF16), wo_ref[...])
    x1 = x_ref[...] + mod_ref[0, 2:3, :] * _rms(mo, post1_ref[...])
    x1_ref[...] = x1
    h2 = _rms(x1, pre2_ref[...]) * (1.0 + mod_ref[0, 4:5, :]) + mod_ref[0, 3:4, :]
    h2t = h2.T
    for nb in range(h2t_ref.shape[0]):
        h2t_ref[nb] = h2t[:, nb * TOKEN_BLOCK:(nb + 1) * TOKEN_BLOCK].astype(BF16)


def _mix(o_f, o_b, zin, yb, x2, mod, onorm, post1, pre2, wa, wb, wo, seq, tm):
    t, d = x2.shape
    hw = o_f.shape[1]
    tps = seq // tm
    zcol = zin.shape[1] // hw - 1
    mgcol = (zin.shape[1] - hw) // (2 * d) - 1
    full = lambda shape: pl.BlockSpec(shape, lambda i: (0,) * len(shape))
    return pl.pallas_call(
        _mix_kernel,
        grid=(t // tm,),
        in_specs=[pl.BlockSpec((tm, hw), lambda i: (i, 0)),
                  pl.BlockSpec((tm, hw), lambda i: (i, 0)),
                  pl.BlockSpec((tm, hw), lambda i: (i, zcol)),
                  pl.BlockSpec((tm, 2 * d), lambda i: (i, mgcol)),
                  pl.BlockSpec((tm, yb.shape[1]), lambda i: (i, 0)),
                  pl.BlockSpec((tm, d), lambda i: (i, 0)),
                  pl.BlockSpec((1, MOD_ROWS, d), lambda i: (i // tps, 0, 0)),
                  full((1, GDN_DV)), full((1, d)), full((1, d)),
                  full(wa.shape), full(wb.shape), full(wo.shape)],
        out_specs=[pl.BlockSpec((tm, d), lambda i: (i, 0)),
                   pl.BlockSpec((tm // TOKEN_BLOCK, d, TOKEN_BLOCK), lambda i: (i, 0, 0))],
        out_shape=[jax.ShapeDtypeStruct((t, d), F32),
                   jax.ShapeDtypeStruct((t // TOKEN_BLOCK, d, TOKEN_BLOCK), BF16)],
        scratch_shapes=[pltpu.VMEM((tm, hw), BF16)],
        compiler_params=_cparams("parallel"),
        name="mix",
    )(o_f, o_b, zin, zin, yb, x2, mod, onorm, post1, pre2, wa, wb, wo)


def _pair_bits(x):
    u = pltpu.bitcast(x.astype(BF16).astype(F32), GATE_BITS)
    return u | (u >> 16)


def _rows_from_pair_bits(row, reps):
    return pltpu.bitcast(jnp.broadcast_to(row, (reps, row.shape[1])), BF16)


def _top_vals(s, k):
    vals = []
    cur = s
    for _ in range(k):
        m = jnp.max(cur, axis=0, keepdims=True)
        vals.append(m)
        cur = jnp.where(cur == m, -jnp.inf, cur)
    return vals


def _peerq_kernel(h2t_ref, wq_ref, keys_ref, a_ref, b_ref, thr_ref):
    qt = _dot(wq_ref[...], h2t_ref[0]).astype(BF16)
    dq = keys_ref.shape[2]
    kk = PEER_TOPK
    a_sorted, b_sorted, b_full = [], [], []
    for h in range(PEER_HEADS):
        s0 = _dot(keys_ref[2 * h], qt[(2 * h) * dq:(2 * h + 1) * dq, :])
        s1 = _dot(keys_ref[2 * h + 1], qt[(2 * h + 1) * dq:(2 * h + 2) * dq, :])
        v0 = _top_vals(s0, kk)
        v1 = _top_vals(s1, kk)
        a_ref[h] = _pair_bits(jnp.exp(s0 - v0[0]))
        b_full.append(jnp.exp(s1 - v1[0]))
        a_sorted.append([jnp.exp(v - v0[0]) for v in v0])
        b_sorted.append([jnp.exp(v - v1[0]) for v in v1])
    a_k = [jnp.concatenate([a_sorted[h][r] for h in range(PEER_HEADS)], axis=0) for r in range(kk)]
    b_k = [jnp.concatenate([b_sorted[h][r] for h in range(PEER_HEADS)], axis=0) for r in range(kk)]
    pairs = [(r0, r1) for r0 in range(kk) for r1 in range(kk // (r0 + 1))]
    cand = [a_k[r0] * b_k[r1] for r0, r1 in pairs]
    cur = list(cand)
    z = jnp.zeros_like(cand[0])
    m = z
    for _ in range(kk):
        m = jnp.maximum(functools.reduce(jnp.maximum, cur), 0.0)
        z = z + m
        cur = [jnp.where(x == m, -1.0, x) for x in cur]
    inv_z = 1.0 / z
    a_b = [a.astype(BF16) for a in a_k]
    b_b = [(b * inv_z).astype(BF16) for b in b_k]
    thr = functools.reduce(jnp.minimum, [
        jnp.where(cand[n] >= m, (a_b[r0] * b_b[r1]).astype(F32), jnp.inf) for n, (r0, r1) in enumerate(pairs)])
    thr_ref[...] = _pair_bits(thr)
    for h in range(PEER_HEADS):
        b_ref[h, 0] = (b_full[h] * inv_z[h:h + 1, :]).astype(BF16)


def _peerq(h2t, wq_t, keys):
    nblk, d, tm = h2t.shape
    t = nblk * tm
    nk = keys.shape[1]
    full = lambda shape: pl.BlockSpec(shape, lambda i: (0,) * len(shape))
    return pl.pallas_call(
        _peerq_kernel,
        grid=(nblk,),
        in_specs=[pl.BlockSpec((1, d, tm), lambda i: (i, 0, 0)), full(wq_t.shape), full(keys.shape)],
        out_specs=[pl.BlockSpec((PEER_HEADS, nk, tm), lambda i: (0, 0, i)),
                   pl.BlockSpec((PEER_HEADS, 1, nk, tm), lambda i: (0, i, 0, 0)),
                   pl.BlockSpec((PEER_HEADS, tm), lambda i: (0, i))],
        out_shape=[jax.ShapeDtypeStruct((PEER_HEADS, nk, t), GATE_BITS),
                   jax.ShapeDtypeStruct((PEER_HEADS, nblk, nk, tm), BF16),
                   jax.ShapeDtypeStruct((PEER_HEADS, t), GATE_BITS)],
        compiler_params=_cparams("parallel"),
        name="peerq",
    )(h2t, wq_t, keys)


def _gelu(x):
    return 0.5 * x * (1.0 + lax.erf(x * (2.0 ** -0.5)))


def _peer_kernel(h2t_ref, u_ref, vt_ref, a_ref, b_ref, thr_ref, ft_ref, act_ref, w2_ref, *, ne, total):
    s = pl.program_id(0)
    nk = a_ref.shape[1]
    nblk, te, tn = act_ref.shape[1:]
    rows = te // nk
    d = ft_ref.shape[1]
    pack = 2 * 8
    halves = 4
    per = rows // halves

    @pl.when(s == 0)
    def _():
        act_ref[...] = jnp.zeros_like(act_ref)
        w2_ref[...] = jnp.zeros_like(w2_ref)

    slot = s % 2
    iv = jnp.clip(s - 1, 0, total - 1) % ne
    first = (jnp.maximum(s - 2, 0) % ne) == 0
    for nb in range(nblk):
        ns = slice(nb * tn, (nb + 1) * tn)
        for r in range(rows):
            if r % per == 0:
                ps = slice((r // per) * (te // halves), (r // per + 1) * (te // halves))
                act_ref[slot, nb, ps, :] = _dot(u_ref[ps, :], h2t_ref[nb])
            wsum = [None] * (nk // pack)
            for h in range(PEER_HEADS):
                a16 = _rows_from_pair_bits(a_ref[h, pl.ds(iv * rows + r, 1), ns], pack // 2)
                t16 = _rows_from_pair_bits(thr_ref[h:h + 1, ns], pack // 2)
                for g in range(nk // pack):
                    p = a16 * b_ref[h, nb, g * pack:(g + 1) * pack, :]
                    term = jnp.where(p >= t16, p, jnp.zeros_like(p))
                    wsum[g] = term if wsum[g] is None else wsum[g] + term
            for g in range(nk // pack):
                gs = slice(r * nk + g * pack, r * nk + (g + 1) * pack)
                w2_ref[slot, nb, gs, :] = _gelu(act_ref[1 - slot, nb, gs, :]).astype(BF16) * wsum[g]
            if r % per == per - 1:
                ds = slice((r // per) * (d // halves), (r // per + 1) * (d // halves))
                part = _dot(vt_ref[ds, :], w2_ref[1 - slot, nb])
                ft_ref[nb, ds, :] = part + jnp.where(first, 0.0, ft_ref[nb, ds, :])


def _peer(h2t, u, vt, a, b, thr, tm, te):
    nblk, d, tn = h2t.shape
    nk = a.shape[1]
    ne = u.shape[0] // te
    tb = tm // tn
    total = (nblk // tb) * ne
    assert te % (2 * nk) == 0
    item = lambda s, lag: jnp.clip(s - lag, 0, total - 1)
    return pl.pallas_call(
        functools.partial(_peer_kernel, ne=ne, total=total),
        grid=(total + 2,),
        in_specs=[pl.BlockSpec((tb, d, tn), lambda s: (item(s, 0) // ne, 0, 0)),
                  pl.BlockSpec((te, d), lambda s: (item(s, 0) % ne, 0)),
                  pl.BlockSpec((d, te), lambda s: (0, item(s, 2) % ne)),
                  pl.BlockSpec((PEER_HEADS, nk, tm), lambda s: (0, 0, item(s, 1) // ne)),
                  pl.BlockSpec((PEER_HEADS, tb, nk, tn), lambda s: (0, item(s, 1) // ne, 0, 0)),
                  pl.BlockSpec((PEER_HEADS, tm), lambda s: (0, item(s, 1) // ne))],
        out_specs=pl.BlockSpec((tb, d, tn), lambda s: (item(s, 2) // ne, 0, 0)),
        out_shape=jax.ShapeDtypeStruct((nblk, d, tn), F32),
        scratch_shapes=[pltpu.VMEM((2, tb, te, tn), F32), pltpu.VMEM((2, tb, te, tn), BF16)],
        compiler_params=_cparams("arbitrary"),
        name="peer",
    )(h2t, u, vt, a, b, thr)


def _final_kernel(ft_ref, x1_ref, mod_ref, post2_ref, out_ref):
    for nb in range(ft_ref.shape[0]):
        rs = slice(nb * TOKEN_BLOCK, (nb + 1) * TOKEN_BLOCK)
        out_ref[rs, :] = x1_ref[rs, :] + mod_ref[0, 5:6, :] * _rms(ft_ref[nb].T, post2_ref[...])


def _final(ft, x1, mod, post2, seq, tm):
    nblk, d, tn = ft.shape
    t = nblk * tn
    tps = seq // tm
    return pl.pallas_call(
        _final_kernel,
        grid=(t // tm,),
        in_specs=[pl.BlockSpec((tm // tn, d, tn), lambda i: (i, 0, 0)),
                  pl.BlockSpec((tm, d), lambda i: (i, 0)),
                  pl.BlockSpec((1, MOD_ROWS, d), lambda i: (i // tps, 0, 0)),
                  pl.BlockSpec((1, d), lambda i: (0, 0))],
        out_specs=pl.BlockSpec((tm, d), lambda i: (i, 0)),
        out_shape=jax.ShapeDtypeStruct((t, d), F32),
        compiler_params=_cparams("parallel"),
        name="final",
    )(ft, x1, mod, post2)


def _tile(n, pref):
    return pref if n % pref == 0 else n


def _layer(x, mod, p):
    bsz, seq, d = x.shape
    t = bsz * seq
    x2 = x.reshape(t, d)
    tm = _tile(seq, 512)
    zin, ab = _inproj(x2, mod, p["pre1"], p["w_main"], p["w_ab"], seq, tm)
    qkvn, kt, yb, gs, aux = _conv(zin, ab, p["conv_qkv"], p["conv_sc"], p["arow"], p["dtrow"], seq, _tile(seq, 256))
    o_f, o_b = _gdn(qkvn, kt, gs, aux, bsz, seq)
    x1, h2t = _mix(o_f, o_b, zin, yb, x2, mod, p["onorm"], p["post1"], p["pre2"], p["wa"], p["wb"], p["wo"], seq, tm)
    a, b, thr = _peerq(h2t, p["wq_t"], p["keys"])
    ft = _peer(h2t, p["u"], p["vt"], a, b, thr, _tile(seq, 1024), 512)
    out = _final(ft, x1, mod, p["post2"], seq, tm)
    return out.reshape(bsz, seq, d)


def _lane_rows(vals_f, vals_b):
    def half(v):
        return jnp.concatenate([jnp.tile(v, 5), jnp.zeros((AB_W // 2 - 5 * GDN_HEADS,), F32)])
    return jnp.concatenate([half(vals_f), half(vals_b)]).reshape(1, AB_W)


def _prep(l, pre_norm1, post_norm1, pre_norm2, post_norm2, w_in, conv_qkv, a_log, dt_bias, out_norm,
          w_branch_a, conv_sc, w_branch_b, w_out, w_peer_q, peer_sub_keys, peer_u, peer_v):
    d = w_in.shape[1]
    hw = GDN_HEADS * GDN_DK
    qkv_w = 2 * hw + GDN_HEADS * GDN_DV
    z_w = GDN_HEADS * GDN_DV
    nh2 = 2 * GDN_HEADS
    sc_w = 3 * conv_sc.shape[2]
    o_z, o_b, o_a, o_sc = qkv_w, qkv_w + z_w, qkv_w + z_w + nh2, qkv_w + z_w + 2 * nh2
    o_mg = o_sc + sc_w
    w = w_in[l]
    w_main = jnp.concatenate([w[:, :qkv_w], w[:, o_sc:o_mg], w[:, o_mg:], w[:, o_z:o_b]], axis=1).astype(BF16)
    beta = w[:, o_b:o_a]
    alpha = w[:, o_a:o_sc]
    pad = jnp.zeros((d, AB_W // 2 - 5 * GDN_HEADS), F32)

    def half(i):
        al = alpha[:, i * GDN_HEADS:(i + 1) * GDN_HEADS]
        return jnp.concatenate([al, beta[:, i * GDN_HEADS:(i + 1) * GDN_HEADS], al, al, al, pad], axis=1)

    w_ab = jnp.concatenate([half(0), half(1)], axis=1).astype(BF16)
    ea = jnp.exp(a_log[l].astype(F32))
    nkeys, dq = peer_sub_keys.shape[3], peer_sub_keys.shape[4]
    return dict(
        pre1=pre_norm1[l].reshape(1, d), post1=post_norm1[l].reshape(1, d),
        pre2=pre_norm2[l].reshape(1, d), post2=post_norm2[l].reshape(1, d),
        w_main=w_main, w_ab=w_ab, conv_qkv=conv_qkv[l], conv_sc=conv_sc[l],
        arow=_lane_rows(ea[0], ea[1]), dtrow=_lane_rows(dt_bias[l, 0], dt_bias[l, 1]),
        onorm=out_norm[l].reshape(1, GDN_DV),
        wa=w_branch_a[l].astype(BF16), wb=w_branch_b[l].astype(BF16), wo=w_out[l].astype(BF16),
        wq_t=w_peer_q[l].T.astype(BF16),
        keys=peer_sub_keys[l].reshape(PEER_HEADS * 2, nkeys, dq).astype(BF16),
        u=peer_u[l].astype(BF16), vt=peer_v[l].T.astype(BF16))


def kernel(x_prompt, x_sample, c_prompt, c_sample, pre_norm1, post_norm1, pre_norm2, post_norm2, w_ada, b_ada, w_in, conv_qkv, a_log, dt_bias, out_norm, w_branch_a, conv_sc, w_branch_b, w_out, w_peer_q, peer_sub_keys, peer_u, peer_v):
    depth = w_in.shape[0]
    d = x_prompt.shape[2]
    nb = x_prompt.shape[0]
    xs = [x_prompt, x_sample]
    c_all = jnp.concatenate([c_prompt, c_sample], axis=0)
    for l in range(depth):
        p = _prep(l, pre_norm1, post_norm1, pre_norm2, post_norm2, w_in, conv_qkv, a_log, dt_bias, out_norm,
                  w_branch_a, conv_sc, w_branch_b, w_out, w_peer_q, peer_sub_keys, peer_u, peer_v)
        mod = _ada(c_all, w_ada[l], b_ada[l]).reshape(c_all.shape[0], N_MOD, d)
        mod = jnp.pad(mod, ((0, 0), (0, MOD_ROWS - N_MOD), (0, 0)))
        xs = [_layer(xs[0], mod[:nb], p), _layer(xs[1], mod[nb:], p)]
    return (xs[0], xs[1])
```

```python
import functools
import math

import jax
import jax.numpy as jnp
from jax import lax
from jax.experimental import pallas as pl
from jax.experimental.pallas import tpu as pltpu

F32 = jnp.float32
BF16 = jnp.bfloat16
GATE_BITS = jnp.uint32

NORM_EPS = 1e-6
N_MOD = 6
MOD_ROWS = 8
GDN_HEADS = 8
GDN_DK = 128
GDN_DV = 128
GDN_CHUNK = 64
PEER_HEADS = 8
PEER_NKEYS = 128
PEER_TOPK = 16
LANES = 128
HALO = 16
AB_W = 128
AUX_ROWS = 24
TOKEN_BLOCK = 256
NEG_BIG = -1e30
VMEM_LIMIT = 56 * 1024 * 1024


def _silu(x):
    return x * jax.nn.sigmoid(x)


def _softplus(x):
    return jnp.maximum(x, 0.0) + jnp.log1p(jnp.exp(-jnp.abs(x)))


def _rms(x, g):
    return x * lax.rsqrt(jnp.mean(x * x, axis=-1, keepdims=True) + NORM_EPS) * g


def _dot(a, b):
    return jnp.dot(a, b, preferred_element_type=F32)


def _cparams(*sem):
    return pltpu.CompilerParams(dimension_semantics=sem, vmem_limit_bytes=VMEM_LIMIT)


def _ada_kernel(c_ref, w_ref, b_ref, o_ref):
    c = c_ref[...]
    o_ref[...] = _dot(_silu(c).astype(BF16), w_ref[...].astype(BF16)) + b_ref[...]


def _ada(c, w, b, tn=1536):
    m, d = c.shape
    n = w.shape[1]
    return pl.pallas_call(
        _ada_kernel,
        grid=(n // tn,),
        in_specs=[pl.BlockSpec((m, d), lambda j: (0, 0)),
                  pl.BlockSpec((d, tn), lambda j: (0, j)),
                  pl.BlockSpec((1, tn), lambda j: (0, j))],
        out_specs=pl.BlockSpec((m, tn), lambda j: (0, j)),
        out_shape=jax.ShapeDtypeStruct((m, n), F32),
        compiler_params=_cparams("parallel"),
        name="ada",
    )(c, w, b.reshape(1, n))


def _inproj_kernel(x_ref, mod_ref, g_ref, w_ref, wab_ref, zin_ref, ab_ref, *, tn):
    y = _rms(x_ref[...], g_ref[...])
    h = y * (1.0 + mod_ref[0, 1:2, :]) + mod_ref[0, 0:1, :]
    hb = h.astype(BF16)
    ab_ref[...] = _dot(hb, wab_ref[...])
    for j in range(w_ref.shape[1] // tn):
        zin_ref[:, j * tn:(j + 1) * tn] = _dot(hb, w_ref[:, j * tn:(j + 1) * tn]).astype(BF16)


def _inproj(x2, mod, g, w_main, w_ab, seq, tm, tn=1024):
    t, d = x2.shape
    n = w_main.shape[1]
    tps = seq // tm
    assert n % tn == 0
    return pl.pallas_call(
        functools.partial(_inproj_kernel, tn=tn),
        grid=(t // tm,),
        in_specs=[pl.BlockSpec((tm, d), lambda i: (i, 0)),
                  pl.BlockSpec((1, MOD_ROWS, d), lambda i: (i // tps, 0, 0)),
                  pl.BlockSpec((1, d), lambda i: (0, 0)),
                  pl.BlockSpec((d, n), lambda i: (0, 0), pipeline_mode=pl.Buffered(1)),
                  pl.BlockSpec((d, AB_W), lambda i: (0, 0))],
        out_specs=[pl.BlockSpec((tm, n), lambda i: (i, 0)),
                   pl.BlockSpec((tm, AB_W), lambda i: (i, 0))],
        out_shape=[jax.ShapeDtypeStruct((t, n), BF16),
                   jax.ShapeDtypeStruct((t, AB_W), F32)],
        compiler_params=_cparams("parallel"),
        name="inproj",
    )(x2, mod, g, w_main, w_ab)


def _neighbour_rows(x, prev_blk, next_blk):
    n = x.shape[0]
    r = lax.broadcasted_iota(jnp.int32, (n, n + HALO), 0)
    c = lax.broadcasted_iota(jnp.int32, (n, n + HALO), 1)
    sel_prev = jnp.where(c == r + (HALO - 1), 1.0, 0.0).astype(x.dtype)
    sel_next = jnp.where(c == r + 1, 1.0, 0.0).astype(x.dtype)
    xm = _dot(sel_prev, jnp.concatenate([prev_blk, x], axis=0))
    xp = _dot(sel_next, jnp.concatenate([x, next_blk], axis=0))
    return xm, xp


def _conv_kernel(qkv_ref, qp_ref, qn_ref, sc_ref, sp_ref, sn_ref, ab_ref, cq_ref, cs_ref,
                 arow_ref, dtrow_ref, qkvn_ref, kt_ref, yb_ref, gs_ref, aux_ref, *, tiles_per_seq):
    i = pl.program_id(0)
    tb = qkv_ref.shape[0]
    nch = tb // GDN_CHUNK
    hw = GDN_HEADS * GDN_DK
    keep_prev = ((i % tiles_per_seq) != 0).astype(BF16)
    keep_next = ((i % tiles_per_seq) != tiles_per_seq - 1).astype(BF16)

    xb = qkv_ref[...]
    xm, xp = _neighbour_rows(xb, qp_ref[...] * keep_prev, qn_ref[...] * keep_next)
    y = _silu(cq_ref[0:1, :] * xm + cq_ref[1:2, :] * xb.astype(F32) + cq_ref[2:3, :] * xp)
    for hh in range(2 * GDN_HEADS):
        seg = y[:, hh * GDN_DK:(hh + 1) * GDN_DK]
        nrm = seg * lax.rsqrt(jnp.sum(seg * seg, axis=-1, keepdims=True) + NORM_EPS)
        if hh < GDN_HEADS:
            nrm = nrm * (GDN_DK ** -0.5)
        qkvn_ref[:, hh * GDN_DK:(hh + 1) * GDN_DK] = nrm.astype(BF16)
        if hh >= GDN_HEADS:
            kt = nrm.T
            for c in range(nch):
                kt_ref[c, (hh - GDN_HEADS) * GDN_DK:(hh - GDN_HEADS + 1) * GDN_DK, :] = (
                    kt[:, c * GDN_CHUNK:(c + 1) * GDN_CHUNK].astype(BF16))
    qkvn_ref[:, 2 * hw:] = y[:, 2 * hw:].astype(BF16)

    w = sc_ref.shape[1] // 3
    xs, cs = slice(0, w), slice(2 * w, 3 * w)
    sxm, sxp = _neighbour_rows(sc_ref[:, xs], sp_ref[:, xs] * keep_prev, sn_ref[:, xs] * keep_next)
    scm, scp = _neighbour_rows(sc_ref[:, cs], sp_ref[:, cs] * keep_prev, sn_ref[:, cs] * keep_next)
    p = sc_ref[:, cs].astype(F32) * sc_ref[:, xs].astype(F32)
    conv = cs_ref[0:1, :] * (scm * sxm) + cs_ref[1:2, :] * p + cs_ref[2:3, :] * (scp * sxp)
    yb_ref[...] = (sc_ref[:, w:2 * w].astype(F32) * conv).astype(BF16)

    ab = ab_ref[...]
    g = -arow_ref[...] * _softplus(ab + dtrow_ref[...])
    rows = lax.broadcasted_iota(jnp.int32, (tb, 1), 0) % GDN_CHUNK
    lane = lax.broadcasted_iota(jnp.int32, (1, AB_W), 1)
    pre, suf = g, g
    sh = 1
    while sh < GDN_CHUNK:
        pre = pre + jnp.where(rows >= sh, pltpu.roll(pre, sh, axis=0), 0.0)
        suf = suf + jnp.where(rows < GDN_CHUNK - sh, pltpu.roll(suf, tb - sh, axis=0), 0.0)
        sh *= 2
    tot = pre + suf - g
    gc = jnp.where(lane >= AB_W // 2, suf, pre)
    grp = (lane % (AB_W // 2)) // GDN_HEADS
    gs = jnp.where(grp == 0, gc,
                   jnp.where(grp == 1, jax.nn.sigmoid(ab),
                             jnp.where(grp == 2, jnp.exp(gc),
                                       jnp.where(grp == 3, jnp.exp(tot - gc), jnp.exp(tot)))))
    gs_ref[...] = gs
    gst = gs.T
    for c in range(nch):
        sl = slice(c * GDN_CHUNK, (c + 1) * GDN_CHUNK)
        for d in range(2):
            base = d * (AB_W // 2)
            blk = jnp.zeros((AUX_ROWS, LANES), F32)
            aux_ref[c, d] = blk
            aux_ref[c, d, 0:8, 0:GDN_CHUNK] = gst[base:base + 8, sl]
            aux_ref[c, d, 8:16, 0:GDN_CHUNK] = gst[base + 24:base + 32, sl]
            aux_ref[c, d, 16:24, :] = jnp.broadcast_to(
                gst[base + 32:base + 40, c * GDN_CHUNK:c * GDN_CHUNK + 1], (8, LANES))


def _conv(zin, ab, conv_qkv, conv_sc, arow, dtrow, seq, tb):
    t = zin.shape[0]
    qw = conv_qkv.shape[1]
    sw = conv_sc.shape[1]
    hw = GDN_HEADS * GDN_DK
    assert qw == 3 * hw and zin.shape[1] % qw == 0
    nt = t // tb
    tps = seq // tb
    hb = tb // HALO
    nhalo = t // HALO
    prev = lambda c: (lambda i: (jnp.maximum(i * hb - 1, 0), c))
    nxt = lambda c: (lambda i: (jnp.minimum((i + 1) * hb, nhalo - 1), c))
    nch = tb // GDN_CHUNK
    return pl.pallas_call(
        functools.partial(_conv_kernel, tiles_per_seq=tps),
        grid=(nt,),
        in_specs=[pl.BlockSpec((tb, qw), lambda i: (i, 0)),
                  pl.BlockSpec((HALO, qw), prev(0)),
                  pl.BlockSpec((HALO, qw), nxt(0)),
                  pl.BlockSpec((tb, qw), lambda i: (i, 1)),
                  pl.BlockSpec((HALO, qw), prev(1)),
                  pl.BlockSpec((HALO, qw), nxt(1)),
                  pl.BlockSpec((tb, AB_W), lambda i: (i, 0)),
                  pl.BlockSpec((3, qw), lambda i: (0, 0)),
                  pl.BlockSpec((3, sw), lambda i: (0, 0)),
                  pl.BlockSpec((1, AB_W), lambda i: (0, 0)),
                  pl.BlockSpec((1, AB_W), lambda i: (0, 0))],
        out_specs=[pl.BlockSpec((tb, qw), lambda i: (i, 0)),
                   pl.BlockSpec((nch, hw, GDN_CHUNK), lambda i: (i, 0, 0)),
                   pl.BlockSpec((tb, sw), lambda i: (i, 0)),
                   pl.BlockSpec((tb, AB_W), lambda i: (i, 0)),
                   pl.BlockSpec((nch, 2, AUX_ROWS, LANES), lambda i: (i, 0, 0, 0))],
        out_shape=[jax.ShapeDtypeStruct((t, qw), BF16),
                   jax.ShapeDtypeStruct((t // GDN_CHUNK, hw, GDN_CHUNK), BF16),
                   jax.ShapeDtypeStruct((t, sw), BF16),
                   jax.ShapeDtypeStruct((t, AB_W), F32),
                   jax.ShapeDtypeStruct((t // GDN_CHUNK, 2, AUX_ROWS, LANES), F32)],
        compiler_params=_cparams("parallel"),
        name="conv",
    )(zin, zin, zin, zin, zin, zin, ab, conv_qkv, conv_sc, arow, dtrow)


def _gdn_kernel(qf_ref, kf_ref, vf_ref, ktf_ref, gsf_ref, auxf_ref,
                qb_ref, kb_ref, vb_ref, ktb_ref, gsb_ref, auxb_ref, of_ref, ob_ref, s_scr):
    @pl.when(pl.program_id(1) == 0)
    def _():
        s_scr[...] = jnp.zeros_like(s_scr)

    c = GDN_CHUNK
    nh = GDN_HEADS
    ri = lax.broadcasted_iota(jnp.int32, (c, c), 0)
    ci = lax.broadcasted_iota(jnp.int32, (c, c), 1)
    incl_d = (ri >= ci, ri <= ci)
    strict_d = (ri > ci, ri < ci)
    refs = ((qf_ref, kf_ref, vf_ref, ktf_ref, gsf_ref, auxf_ref, of_ref),
            (qb_ref, kb_ref, vb_ref, ktb_ref, gsb_ref, auxb_ref, ob_ref))
    units = [(d, h) for d in range(2) for h in range(nh)]
    every = lambda f: [f(n, d, h) for n, (d, h) in enumerate(units)]
    hsl = lambda h: slice(h * GDN_DK, (h + 1) * GDN_DK)
    col = lambda d, j: slice(d * (AB_W // 2) + j, d * (AB_W // 2) + j + 1)

    gsv = [gsf_ref[...], gsb_ref[...]]
    q = every(lambda n, d, h: refs[d][0][:, hsl(h)])
    k = every(lambda n, d, h: refs[d][1][:, hsl(h)].astype(F32))
    v = every(lambda n, d, h: refs[d][2][:, hsl(h)].astype(F32))
    kt = every(lambda n, d, h: refs[d][3][0, hsl(h), :])
    gcol = every(lambda n, d, h: gsv[d][:, col(d, h)])
    beta = every(lambda n, d, h: gsv[d][:, col(d, nh + h)])
    egc = every(lambda n, d, h: gsv[d][:, col(d, 2 * nh + h)])
    grow = every(lambda n, d, h: refs[d][5][0, 0, h:h + 1, 0:c])
    kdrow = every(lambda n, d, h: refs[d][5][0, 0, nh + h:nh + h + 1, 0:c])
    etot = every(lambda n, d, h: refs[d][5][0, 0, 2 * nh + h:2 * nh + h + 1, :])

    decay = every(lambda n, d, h: jnp.exp(jnp.where(incl_d[d], gcol[n] - grow[n], NEG_BIG)))
    kb = every(lambda n, d, h: k[n] * beta[n])
    gram = every(lambda n, d, h: _dot(jnp.concatenate([kb[n].astype(BF16), q[n]], axis=0), kt[n]))
    low = every(lambda n, d, h: jnp.where(strict_d[d], gram[n][:c] * decay[n], 0.0))
    attn = every(lambda n, d, h: jnp.where(incl_d[d], gram[n][c:] * decay[n], 0.0).astype(BF16))

    x = every(lambda n, d, h: jnp.concatenate([v[n] * beta[n], kb[n] * egc[n]], axis=-1))
    lp = [l.astype(BF16) for l in low]
    x = every(lambda n, d, h: x[n] - _dot(lp[n], x[n].astype(BF16)))
    p = 2
    while p < c:
        lp = [_dot(l, l).astype(BF16) for l in lp]
        x = every(lambda n, d, h: x[n] + _dot(lp[n], x[n].astype(BF16)))
        p *= 2

    s = [s_scr[n] for n in range(len(units))]
    sb = [t.astype(BF16) for t in s]
    qg = every(lambda n, d, h: (q[n].astype(F32) * egc[n]).astype(BF16))
    ws = every(lambda n, d, h: _dot(jnp.concatenate([x[n][:, GDN_DV:].astype(BF16), qg[n]], axis=0), sb[n]))
    vnb = every(lambda n, d, h: (x[n][:, :GDN_DV] - ws[n][:c]).astype(BF16))
    o = every(lambda n, d, h: ws[n][c:] + _dot(attn[n], vnb[n]))
    for n, (d, h) in enumerate(units):
        refs[d][6][:, hsl(h)] = o[n].astype(BF16)
    ktd = every(lambda n, d, h: (kt[n].astype(F32) * kdrow[n]).astype(BF16))
    for n in range(len(units)):
        s_scr[n] = s[n] * etot[n] + _dot(ktd[n], vnb[n])


def _gdn(qkvn, kt, gs, aux, batch, seq):
    t = qkvn.shape[0]
    hw = GDN_HEADS * GDN_DK
    c = GDN_CHUNK
    nc = seq // c
    fwd = lambda b, j: b * nc + j
    bwd = lambda b, j: b * nc + nc - 1 - j

    def specs(ch, d):
        return [pl.BlockSpec((c, hw), lambda b, j: (ch(b, j), 0)),
                pl.BlockSpec((c, hw), lambda b, j: (ch(b, j), 1)),
                pl.BlockSpec((c, hw), lambda b, j: (ch(b, j), 2)),
                pl.BlockSpec((1, hw, c), lambda b, j: (ch(b, j), 0, 0)),
                pl.BlockSpec((c, AB_W), lambda b, j: (ch(b, j), 0)),
                pl.BlockSpec((1, 1, AUX_ROWS, LANES), lambda b, j: (ch(b, j), d, 0, 0))]

    return pl.pallas_call(
        _gdn_kernel,
        grid=(batch, nc),
        in_specs=specs(fwd, 0) + specs(bwd, 1),
        out_specs=[pl.BlockSpec((c, hw), lambda b, j: (fwd(b, j), 0)),
                   pl.BlockSpec((c, hw), lambda b, j: (bwd(b, j), 0))],
        out_shape=[jax.ShapeDtypeStruct((t, hw), BF16), jax.ShapeDtypeStruct((t, hw), BF16)],
        scratch_shapes=[pltpu.VMEM((2 * GDN_HEADS, GDN_DK, GDN_DV), F32)],
        compiler_params=_cparams("parallel", "arbitrary"),
        name="gdn",
    )(qkvn, qkvn, qkvn, kt, gs, aux, qkvn, qkvn, qkvn, kt, gs, aux)


def _mix_kernel(of_ref, ob_ref, z_ref, mg_ref, yb_ref, x_ref, mod_ref, onorm_ref, post1_ref, pre2_ref,
                wa_ref, wb_ref, wo_ref, x1_ref, h2t_ref, gated_scr):
    o = of_ref[...].astype(F32) + ob_ref[...].astype(F32)
    z = z_ref[...].astype(F32)
    for h in range(GDN_HEADS):
        hs = slice(h * GDN_DV, (h + 1) * GDN_DV)
        gated_scr[:, hs] = (_rms(o[:, hs], onorm_ref[...]) * _silu(z[:, hs])).astype(BF16)
    ya = _dot(gated_scr[...], wa_ref[...])
    yb = _dot(yb_ref[...], wb_ref[...])
    d = ya.shape[1]
    mg = mg_ref[...].astype(F32)
    mix = jax.nn.sigmoid(mg[:, :d]) * ya + jax.nn.sigmoid(mg[:, d:]) * yb
    mo = _dot(mix.astype(BF16), wo_ref[...])
    x1 = x_ref[...] + mod_ref[0, 2:3, :] * _rms(mo, post1_ref[...])
    x1_ref[...] = x1
    h2 = _rms(x1, pre2_ref[...]) * (1.0 + mod_ref[0, 4:5, :]) + mod_ref[0, 3:4, :]
    h2t = h2.T
    for nb in range(h2t_ref.shape[0]):
        h2t_ref[nb] = h2t[:, nb * TOKEN_BLOCK:(nb + 1) * TOKEN_BLOCK].astype(BF16)


def _mix(o_f, o_b, zin, yb, x2, mod, onorm, post1, pre2, wa, wb, wo, seq, tm):
    t, d = x2.shape
    hw = o_f.shape[1]
    tps = seq // tm
    zcol = zin.shape[1] // hw - 1
    mgcol = (zin.shape[1] - hw) // (2 * d) - 1
    full = lambda shape: pl.BlockSpec(shape, lambda i: (0,) * len(shape))
    return pl.pallas_call(
        _mix_kernel,
        grid=(t // tm,),
        in_specs=[pl.BlockSpec((tm, hw), lambda i: (i, 0)),
                  pl.BlockSpec((tm, hw), lambda i: (i, 0)),
                  pl.BlockSpec((tm, hw), lambda i: (i, zcol)),
                  pl.BlockSpec((tm, 2 * d), lambda i: (i, mgcol)),
                  pl.BlockSpec((tm, yb.shape[1]), lambda i: (i, 0)),
                  pl.BlockSpec((tm, d), lambda i: (i, 0)),
                  pl.BlockSpec((1, MOD_ROWS, d), lambda i: (i // tps, 0, 0)),
                  full((1, GDN_DV)), full((1, d)), full((1, d)),
                  full(wa.shape), full(wb.shape), full(wo.shape)],
        out_specs=[pl.BlockSpec((tm, d), lambda i: (i, 0)),
                   pl.BlockSpec((tm // TOKEN_BLOCK, d, TOKEN_BLOCK), lambda i: (i, 0, 0))],
        out_shape=[jax.ShapeDtypeStruct((t, d), F32),
                   jax.ShapeDtypeStruct((t // TOKEN_BLOCK, d, TOKEN_BLOCK), BF16)],
        scratch_shapes=[pltpu.VMEM((tm, hw), BF16)],
        compiler_params=_cparams("parallel"),
        name="mix",
    )(o_f, o_b, zin, zin, yb, x2, mod, onorm, post1, pre2, wa, wb, wo)


def _pair_bits(x):
    u = pltpu.bitcast(x.astype(BF16).astype(F32), GATE_BITS)
    return u | (u >> 16)


def _rows_from_pair_bits(row, reps):
    return pltpu.bitcast(jnp.broadcast_to(row, (reps, row.shape[1])), BF16)


def _merge_sort_pairs(lo, hi):
    def merge(lo, hi, r):
        step = 2 * r
        if step < hi - lo:
            yield from merge(lo, hi, step)
            yield from merge(lo + r, hi, step)
            yield from ((i, i + r) for i in range(lo + r, hi - r, step))
        else:
            yield (lo, lo + r)

    if hi > lo:
        mid = lo + (hi - lo) // 2
        yield from _merge_sort_pairs(lo, mid)
        yield from _merge_sort_pairs(mid + 1, hi)
        yield from merge(lo, hi, 1)


def _top_vals(s, k):
    sub = 8
    assert s.shape[0] == k * sub and k & (k - 1) == 0
    c = [s[i * sub:(i + 1) * sub, :] for i in range(k)]

    def exchange(i, j):
        c[i], c[j] = jnp.maximum(c[i], c[j]), jnp.minimum(c[i], c[j])

    for i, j in _merge_sort_pairs(0, k - 1):
        exchange(i, j)
    shift = sub // 2
    while shift:
        other = [pltpu.roll(x, shift, axis=0) for x in c]
        c = [jnp.maximum(c[i], other[k - 1 - i]) for i in range(k)]
        dist = k // 2
        while dist:
            for i in range(k):
                if i & dist == 0:
                    exchange(i, i + dist)
            dist //= 2
        shift //= 2
    return [x[0:1, :] for x in c]


def _peerq_kernel(h2t_ref, wq_ref, keys_ref, a_ref, b_ref, thr_ref):
    qt = _dot(wq_ref[...], h2t_ref[0]).astype(BF16)
    dq = keys_ref.shape[2]
    kk = PEER_TOPK
    a_sorted, b_sorted, b_full = [], [], []
    for h in range(PEER_HEADS):
        s0 = _dot(keys_ref[2 * h], qt[(2 * h) * dq:(2 * h + 1) * dq, :])
        s1 = _dot(keys_ref[2 * h + 1], qt[(2 * h + 1) * dq:(2 * h + 2) * dq, :])
        v0 = _top_vals(s0, kk)
        v1 = _top_vals(s1, kk)
        a_ref[h] = _pair_bits(jnp.exp(s0 - v0[0]))
        b_full.append(jnp.exp(s1 - v1[0]))
        a_sorted.append([jnp.exp(v - v0[0]) for v in v0])
        b_sorted.append([jnp.exp(v - v1[0]) for v in v1])
    a_k = [jnp.concatenate([a_sorted[h][r] for h in range(PEER_HEADS)], axis=0) for r in range(kk)]
    b_k = [jnp.concatenate([b_sorted[h][r] for h in range(PEER_HEADS)], axis=0) for r in range(kk)]
    pairs = [(r0, r1) for r0 in range(kk) for r1 in range(kk // (r0 + 1))]
    cand = [a_k[r0] * b_k[r1] for r0, r1 in pairs]
    cur = list(cand)
    z = jnp.zeros_like(cand[0])
    m = z
    for _ in range(kk):
        m = jnp.maximum(functools.reduce(jnp.maximum, cur), 0.0)
        z = z + m
        cur = [jnp.where(x == m, -1.0, x) for x in cur]
    inv_z = 1.0 / z
    a_b = [a.astype(BF16) for a in a_k]
    b_b = [(b * inv_z).astype(BF16) for b in b_k]
    thr = functools.reduce(jnp.minimum, [
        jnp.where(cand[n] >= m, (a_b[r0] * b_b[r1]).astype(F32), jnp.inf) for n, (r0, r1) in enumerate(pairs)])
    thr_ref[...] = _pair_bits(thr)
    for h in range(PEER_HEADS):
        b_ref[h, 0] = (b_full[h] * inv_z[h:h + 1, :]).astype(BF16)


def _peerq(h2t, wq_t, keys):
    nblk, d, tm = h2t.shape
    t = nblk * tm
    nk = keys.shape[1]
    full = lambda shape: pl.BlockSpec(shape, lambda i: (0,) * len(shape))
    return pl.pallas_call(
        _peerq_kernel,
        grid=(nblk,),
        in_specs=[pl.BlockSpec((1, d, tm), lambda i: (i, 0, 0)), full(wq_t.shape), full(keys.shape)],
        out_specs=[pl.BlockSpec((PEER_HEADS, nk, tm), lambda i: (0, 0, i)),
                   pl.BlockSpec((PEER_HEADS, 1, nk, tm), lambda i: (0, i, 0, 0)),
                   pl.BlockSpec((PEER_HEADS, tm), lambda i: (0, i))],
        out_shape=[jax.ShapeDtypeStruct((PEER_HEADS, nk, t), GATE_BITS),
                   jax.ShapeDtypeStruct((PEER_HEADS, nblk, nk, tm), BF16),
                   jax.ShapeDtypeStruct((PEER_HEADS, t), GATE_BITS)],
        compiler_params=_cparams("parallel"),
        name="peerq",
    )(h2t, wq_t, keys)


def _gelu(x):
    return 0.5 * x * (1.0 + lax.erf(x * (2.0 ** -0.5)))


def _peer_kernel(h2t_ref, u_ref, vt_ref, a_ref, b_ref, thr_ref, ft_ref, act0_ref, act1_ref, w20_ref, w21_ref,
                 *, ne, total):
    s = pl.program_id(0)
    nk = a_ref.shape[1]
    nblk, te, tn = act0_ref.shape
    rows = te // nk
    d = ft_ref.shape[1]
    pack = 2 * 8
    a_pieces, v_pieces = 4, 4

    @pl.when(s == 0)
    def _():
        for ref in (act0_ref, act1_ref, w20_ref, w21_ref):
            ref[...] = jnp.zeros_like(ref)

    iv = jnp.clip(s - 1, 0, total - 1) % ne

    @pl.when((jnp.maximum(s - 2, 0) % ne) == 0)
    def _():
        ft_ref[...] = jnp.zeros_like(ft_ref)

    def body(act_w, act_r, w2_w, w2_r):
        for nb in range(nblk):
            ns = slice(nb * tn, (nb + 1) * tn)
            for r in range(rows):
                if r % (rows // a_pieces) == 0:
                    pa = r // (rows // a_pieces)
                    ps = slice(pa * (te // a_pieces), (pa + 1) * (te // a_pieces))
                    act_w[nb, ps, :] = _dot(u_ref[ps, :], h2t_ref[nb])
                a16 = [_rows_from_pair_bits(a_ref[h, pl.ds(iv * rows + r, 1), ns], pack // 2)
                       for h in range(PEER_HEADS)]
                t16 = [_rows_from_pair_bits(thr_ref[h:h + 1, ns], pack // 2) for h in range(PEER_HEADS)]
                for g in range(nk // pack):
                    gs = slice(r * nk + g * pack, r * nk + (g + 1) * pack)
                    wsum = None
                    for h in range(PEER_HEADS):
                        p = a16[h] * b_ref[h, nb, g * pack:(g + 1) * pack, :]
                        term = jnp.where(p >= t16[h], p, jnp.zeros_like(p))
                        wsum = term if wsum is None else wsum + term
                    w2_w[nb, gs, :] = _gelu(act_r[nb, gs, :].astype(BF16)) * wsum
                if (r + 1) % (rows // v_pieces) == 0:
                    pv = r // (rows // v_pieces)
                    ds = slice(pv * (d // v_pieces), (pv + 1) * (d // v_pieces))
                    ft_ref[nb, ds, :] += _dot(vt_ref[ds, :], w2_r[nb])

    @pl.when(s % 2 == 0)
    def _():
        body(act0_ref, act1_ref, w20_ref, w21_ref)

    @pl.when(s % 2 == 1)
    def _():
        body(act1_ref, act0_ref, w21_ref, w20_ref)


def _peer(h2t, u, vt, a, b, thr, tm, te):
    nblk, d, tn = h2t.shape
    nk = a.shape[1]
    ne = u.shape[0] // te
    tb = tm // tn
    total = (nblk // tb) * ne
    assert te % (2 * nk) == 0
    item = lambda s, lag: jnp.clip(s - lag, 0, total - 1)
    return pl.pallas_call(
        functools.partial(_peer_kernel, ne=ne, total=total),
        grid=(total + 2,),
        in_specs=[pl.BlockSpec((tb, d, tn), lambda s: (item(s, 0) // ne, 0, 0)),
                  pl.BlockSpec((te, d), lambda s: (item(s, 0) % ne, 0)),
                  pl.BlockSpec((d, te), lambda s: (0, item(s, 2) % ne)),
                  pl.BlockSpec((PEER_HEADS, nk, tm), lambda s: (0, 0, item(s, 1) // ne)),
                  pl.BlockSpec((PEER_HEADS, tb, nk, tn), lambda s: (0, item(s, 1) // ne, 0, 0)),
                  pl.BlockSpec((PEER_HEADS, tm), lambda s: (0, item(s, 1) // ne))],
        out_specs=pl.BlockSpec((tb, d, tn), lambda s: (item(s, 2) // ne, 0, 0)),
        out_shape=jax.ShapeDtypeStruct((nblk, d, tn), F32),
        scratch_shapes=[pltpu.VMEM((tb, te, tn), F32), pltpu.VMEM((tb, te, tn), F32),
                        pltpu.VMEM((tb, te, tn), BF16), pltpu.VMEM((tb, te, tn), BF16)],
        compiler_params=_cparams("arbitrary"),
        name="peer",
    )(h2t, u, vt, a, b, thr)


def _final_kernel(ft_ref, x1_ref, mod_ref, post2_ref, out_ref):
    for nb in range(ft_ref.shape[0]):
        rs = slice(nb * TOKEN_BLOCK, (nb + 1) * TOKEN_BLOCK)
        out_ref[rs, :] = x1_ref[rs, :] + mod_ref[0, 5:6, :] * _rms(ft_ref[nb].T, post2_ref[...])


def _final(ft, x1, mod, post2, seq, tm):
    nblk, d, tn = ft.shape
    t = nblk * tn
    tps = seq // tm
    return pl.pallas_call(
        _final_kernel,
        grid=(t // tm,),
        in_specs=[pl.BlockSpec((tm // tn, d, tn), lambda i: (i, 0, 0)),
                  pl.BlockSpec((tm, d), lambda i: (i, 0)),
                  pl.BlockSpec((1, MOD_ROWS, d), lambda i: (i // tps, 0, 0)),
                  pl.BlockSpec((1, d), lambda i: (0, 0))],
        out_specs=pl.BlockSpec((tm, d), lambda i: (i, 0)),
        out_shape=jax.ShapeDtypeStruct((t, d), F32),
        compiler_params=_cparams("parallel"),
        name="final",
    )(ft, x1, mod, post2)


def _tile(n, pref):
    return pref if n % pref == 0 else n


def _layer(x, mod, p):
    bsz, seq, d = x.shape
    t = bsz * seq
    x2 = x.reshape(t, d)
    tm = _tile(seq, 512)
    zin, ab = _inproj(x2, mod, p["pre1"], p["w_main"], p["w_ab"], seq, tm)
    qkvn, kt, yb, gs, aux = _conv(zin, ab, p["conv_qkv"], p["conv_sc"], p["arow"], p["dtrow"], seq, _tile(seq, 256))
    o_f, o_b = _gdn(qkvn, kt, gs, aux, bsz, seq)
    x1, h2t = _mix(o_f, o_b, zin, yb, x2, mod, p["onorm"], p["post1"], p["pre2"], p["wa"], p["wb"], p["wo"], seq, tm)
    a, b, thr = _peerq(h2t, p["wq_t"], p["keys"])
    ft = _peer(h2t, p["u"], p["vt"], a, b, thr, _tile(seq, 1024), 1024)
    out = _final(ft, x1, mod, p["post2"], seq, tm)
    return out.reshape(bsz, seq, d)


def _lane_rows(vals_f, vals_b):
    def half(v):
        return jnp.concatenate([jnp.tile(v, 5), jnp.zeros((AB_W // 2 - 5 * GDN_HEADS,), F32)])
    return jnp.concatenate([half(vals_f), half(vals_b)]).reshape(1, AB_W)


def _prep(l, pre_norm1, post_norm1, pre_norm2, post_norm2, w_in, conv_qkv, a_log, dt_bias, out_norm,
          w_branch_a, conv_sc, w_branch_b, w_out, w_peer_q, peer_sub_keys, peer_u, peer_v):
    d = w_in.shape[1]
    hw = GDN_HEADS * GDN_DK
    qkv_w = 2 * hw + GDN_HEADS * GDN_DV
    z_w = GDN_HEADS * GDN_DV
    nh2 = 2 * GDN_HEADS
    sc_w = 3 * conv_sc.shape[2]
    o_z, o_b, o_a, o_sc = qkv_w, qkv_w + z_w, qkv_w + z_w + nh2, qkv_w + z_w + 2 * nh2
    o_mg = o_sc + sc_w
    w = w_in[l]
    w_main = jnp.concatenate([w[:, :qkv_w], w[:, o_sc:o_mg], w[:, o_mg:], w[:, o_z:o_b]], axis=1).astype(BF16)
    beta = w[:, o_b:o_a]
    alpha = w[:, o_a:o_sc]
    pad = jnp.zeros((d, AB_W // 2 - 5 * GDN_HEADS), F32)

    def half(i):
        al = alpha[:, i * GDN_HEADS:(i + 1) * GDN_HEADS]
        return jnp.concatenate([al, beta[:, i * GDN_HEADS:(i + 1) * GDN_HEADS], al, al, al, pad], axis=1)

    w_ab = jnp.concatenate([half(0), half(1)], axis=1).astype(BF16)
    ea = jnp.exp(a_log[l].astype(F32))
    nkeys, dq = peer_sub_keys.shape[3], peer_sub_keys.shape[4]
    return dict(
        pre1=pre_norm1[l].reshape(1, d), post1=post_norm1[l].reshape(1, d),
        pre2=pre_norm2[l].reshape(1, d), post2=post_norm2[l].reshape(1, d),
        w_main=w_main, w_ab=w_ab, conv_qkv=conv_qkv[l], conv_sc=conv_sc[l],
        arow=_lane_rows(ea[0], ea[1]), dtrow=_lane_rows(dt_bias[l, 0], dt_bias[l, 1]),
        onorm=out_norm[l].reshape(1, GDN_DV),
        wa=w_branch_a[l].astype(BF16), wb=w_branch_b[l].astype(BF16), wo=w_out[l].astype(BF16),
        wq_t=w_peer_q[l].T.astype(BF16),
        keys=peer_sub_keys[l].reshape(PEER_HEADS * 2, nkeys, dq).astype(BF16),
        u=peer_u[l].astype(BF16), vt=peer_v[l].T.astype(BF16))


def kernel(x_prompt, x_sample, c_prompt, c_sample, pre_norm1, post_norm1, pre_norm2, post_norm2, w_ada, b_ada, w_in, conv_qkv, a_log, dt_bias, out_norm, w_branch_a, conv_sc, w_branch_b, w_out, w_peer_q, peer_sub_keys, peer_u, peer_v):
    depth = w_in.shape[0]
    d = x_prompt.shape[2]
    nb = x_prompt.shape[0]
    xs = [x_prompt, x_sample]
    c_all = jnp.concatenate([c_prompt, c_sample], axis=0)
    for l in range(depth):
        p = _prep(l, pre_norm1, post_norm1, pre_norm2, post_norm2, w_in, conv_qkv, a_log, dt_bias, out_norm,
                  w_branch_a, conv_sc, w_branch_b, w_out, w_peer_q, peer_sub_keys, peer_u, peer_v)
        mod = _ada(c_all, w_ada[l], b_ada[l]).reshape(c_all.shape[0], N_MOD, d)
        mod = jnp.pad(mod, ((0, 0), (0, MOD_ROWS - N_MOD), (0, 0)))
        xs = [_layer(xs[0], mod[:nb], p), _layer(xs[1], mod[nb:], p)]
    return (xs[0], xs[1])
```

```python
import functools
import math

import jax
import jax.numpy as jnp
from jax import lax
from jax.experimental import pallas as pl
from jax.experimental.pallas import tpu as pltpu

F32 = jnp.float32
BF16 = jnp.bfloat16
GATE_BITS = jnp.uint32

NORM_EPS = 1e-6
N_MOD = 6
MOD_ROWS = 8
GDN_HEADS = 8
GDN_DK = 128
GDN_DV = 128
GDN_CHUNK = 64
PEER_HEADS = 8
PEER_NKEYS = 128
PEER_TOPK = 16
LANES = 128
HALO = 16
AB_W = 128
AUX_ROWS = 24
TOKEN_BLOCK = 256
NEG_BIG = -1e30
VMEM_LIMIT = 56 * 1024 * 1024


def _silu(x):
    return x * jax.nn.sigmoid(x)


def _softplus(x):
    return jnp.maximum(x, 0.0) + jnp.log1p(jnp.exp(-jnp.abs(x)))


def _rms(x, g):
    return x * lax.rsqrt(jnp.mean(x * x, axis=-1, keepdims=True) + NORM_EPS) * g


def _dot(a, b):
    return jnp.dot(a, b, preferred_element_type=F32)


def _cparams(*sem):
    return pltpu.CompilerParams(dimension_semantics=sem, vmem_limit_bytes=VMEM_LIMIT)


def _ada_kernel(c_ref, w_ref, b_ref, o_ref):
    c = c_ref[...]
    o_ref[...] = _dot(_silu(c).astype(BF16), w_ref[...].astype(BF16)) + b_ref[...]


def _ada(c, w, b, tn=1536):
    m, d = c.shape
    n = w.shape[1]
    return pl.pallas_call(
        _ada_kernel,
        grid=(n // tn,),
        in_specs=[pl.BlockSpec((m, d), lambda j: (0, 0)),
                  pl.BlockSpec((d, tn), lambda j: (0, j)),
                  pl.BlockSpec((1, tn), lambda j: (0, j))],
        out_specs=pl.BlockSpec((m, tn), lambda j: (0, j)),
        out_shape=jax.ShapeDtypeStruct((m, n), F32),
        compiler_params=_cparams("parallel"),
        name="ada",
    )(c, w, b.reshape(1, n))


def _inproj_kernel(x_ref, mod_ref, g_ref, w_ref, wab_ref, zin_ref, ab_ref, *, tn):
    y = _rms(x_ref[...], g_ref[...])
    h = y * (1.0 + mod_ref[0, 1:2, :]) + mod_ref[0, 0:1, :]
    hb = h.astype(BF16)
    ab_ref[...] = _dot(hb, wab_ref[...])
    for j in range(w_ref.shape[1] // tn):
        zin_ref[:, j * tn:(j + 1) * tn] = _dot(hb, w_ref[:, j * tn:(j + 1) * tn]).astype(BF16)


def _inproj(x2, mod, g, w_main, w_ab, seq, tm, tn=1024):
    t, d = x2.shape
    n = w_main.shape[1]
    tps = seq // tm
    assert n % tn == 0
    return pl.pallas_call(
        functools.partial(_inproj_kernel, tn=tn),
        grid=(t // tm,),
        in_specs=[pl.BlockSpec((tm, d), lambda i: (i, 0)),
                  pl.BlockSpec((1, MOD_ROWS, d), lambda i: (i // tps, 0, 0)),
                  pl.BlockSpec((1, d), lambda i: (0, 0)),
                  pl.BlockSpec((d, n), lambda i: (0, 0), pipeline_mode=pl.Buffered(1)),
                  pl.BlockSpec((d, AB_W), lambda i: (0, 0))],
        out_specs=[pl.BlockSpec((tm, n), lambda i: (i, 0)),
                   pl.BlockSpec((tm, AB_W), lambda i: (i, 0))],
        out_shape=[jax.ShapeDtypeStruct((t, n), BF16),
                   jax.ShapeDtypeStruct((t, AB_W), F32)],
        compiler_params=_cparams("parallel"),
        name="inproj",
    )(x2, mod, g, w_main, w_ab)


def _neighbour_rows(x, prev_blk, next_blk):
    n = x.shape[0]
    r = lax.broadcasted_iota(jnp.int32, (n, n + HALO), 0)
    c = lax.broadcasted_iota(jnp.int32, (n, n + HALO), 1)
    sel_prev = jnp.where(c == r + (HALO - 1), 1.0, 0.0).astype(x.dtype)
    sel_next = jnp.where(c == r + 1, 1.0, 0.0).astype(x.dtype)
    xm = _dot(sel_prev, jnp.concatenate([prev_blk, x], axis=0))
    xp = _dot(sel_next, jnp.concatenate([x, next_blk], axis=0))
    return xm, xp


def _conv_kernel(qkv_ref, qp_ref, qn_ref, sc_ref, sp_ref, sn_ref, ab_ref, cq_ref, cs_ref,
                 arow_ref, dtrow_ref, qkvn_ref, kt_ref, yb_ref, gs_ref, aux_ref, *, tiles_per_seq):
    i = pl.program_id(0)
    tb = qkv_ref.shape[0]
    nch = tb // GDN_CHUNK
    hw = GDN_HEADS * GDN_DK
    keep_prev = ((i % tiles_per_seq) != 0).astype(BF16)
    keep_next = ((i % tiles_per_seq) != tiles_per_seq - 1).astype(BF16)

    xb = qkv_ref[...]
    xm, xp = _neighbour_rows(xb, qp_ref[...] * keep_prev, qn_ref[...] * keep_next)
    y = _silu(cq_ref[0:1, :] * xm + cq_ref[1:2, :] * xb.astype(F32) + cq_ref[2:3, :] * xp)
    for hh in range(2 * GDN_HEADS):
        seg = y[:, hh * GDN_DK:(hh + 1) * GDN_DK]
        nrm = seg * lax.rsqrt(jnp.sum(seg * seg, axis=-1, keepdims=True) + NORM_EPS)
        if hh < GDN_HEADS:
            nrm = nrm * (GDN_DK ** -0.5)
        qkvn_ref[:, hh * GDN_DK:(hh + 1) * GDN_DK] = nrm.astype(BF16)
        if hh >= GDN_HEADS:
            kt = nrm.T
            for c in range(nch):
                kt_ref[c, (hh - GDN_HEADS) * GDN_DK:(hh - GDN_HEADS + 1) * GDN_DK, :] = (
                    kt[:, c * GDN_CHUNK:(c + 1) * GDN_CHUNK].astype(BF16))
    qkvn_ref[:, 2 * hw:] = y[:, 2 * hw:].astype(BF16)

    w = sc_ref.shape[1] // 3
    xs, cs = slice(0, w), slice(2 * w, 3 * w)
    sxm, sxp = _neighbour_rows(sc_ref[:, xs], sp_ref[:, xs] * keep_prev, sn_ref[:, xs] * keep_next)
    scm, scp = _neighbour_rows(sc_ref[:, cs], sp_ref[:, cs] * keep_prev, sn_ref[:, cs] * keep_next)
    p = sc_ref[:, cs].astype(F32) * sc_ref[:, xs].astype(F32)
    conv = cs_ref[0:1, :] * (scm * sxm) + cs_ref[1:2, :] * p + cs_ref[2:3, :] * (scp * sxp)
    yb_ref[...] = (sc_ref[:, w:2 * w].astype(F32) * conv).astype(BF16)

    ab = ab_ref[...]
    g = -arow_ref[...] * _softplus(ab + dtrow_ref[...])
    rows = lax.broadcasted_iota(jnp.int32, (tb, 1), 0) % GDN_CHUNK
    lane = lax.broadcasted_iota(jnp.int32, (1, AB_W), 1)
    pre, suf = g, g
    sh = 1
    while sh < GDN_CHUNK:
        pre = pre + jnp.where(rows >= sh, pltpu.roll(pre, sh, axis=0), 0.0)
        suf = suf + jnp.where(rows < GDN_CHUNK - sh, pltpu.roll(suf, tb - sh, axis=0), 0.0)
        sh *= 2
    tot = pre + suf - g
    gc = jnp.where(lane >= AB_W // 2, suf, pre)
    grp = (lane % (AB_W // 2)) // GDN_HEADS
    gs = jnp.where(grp == 0, gc,
                   jnp.where(grp == 1, jax.nn.sigmoid(ab),
                             jnp.where(grp == 2, jnp.exp(gc),
                                       jnp.where(grp == 3, jnp.exp(tot - gc), jnp.exp(tot)))))
    gs_ref[...] = gs
    gst = gs.T
    for c in range(nch):
        sl = slice(c * GDN_CHUNK, (c + 1) * GDN_CHUNK)
        for d in range(2):
            base = d * (AB_W // 2)
            blk = jnp.zeros((AUX_ROWS, LANES), F32)
            aux_ref[c, d] = blk
            aux_ref[c, d, 0:8, 0:GDN_CHUNK] = gst[base:base + 8, sl]
            aux_ref[c, d, 8:16, 0:GDN_CHUNK] = gst[base + 24:base + 32, sl]
            aux_ref[c, d, 16:24, :] = jnp.broadcast_to(
                gst[base + 32:base + 40, c * GDN_CHUNK:c * GDN_CHUNK + 1], (8, LANES))


def _conv(zin, ab, conv_qkv, conv_sc, arow, dtrow, seq, tb):
    t = zin.shape[0]
    qw = conv_qkv.shape[1]
    sw = conv_sc.shape[1]
    hw = GDN_HEADS * GDN_DK
    assert qw == 3 * hw and zin.shape[1] % qw == 0
    nt = t // tb
    tps = seq // tb
    hb = tb // HALO
    nhalo = t // HALO
    prev = lambda c: (lambda i: (jnp.maximum(i * hb - 1, 0), c))
    nxt = lambda c: (lambda i: (jnp.minimum((i + 1) * hb, nhalo - 1), c))
    nch = tb // GDN_CHUNK
    return pl.pallas_call(
        functools.partial(_conv_kernel, tiles_per_seq=tps),
        grid=(nt,),
        in_specs=[pl.BlockSpec((tb, qw), lambda i: (i, 0)),
                  pl.BlockSpec((HALO, qw), prev(0)),
                  pl.BlockSpec((HALO, qw), nxt(0)),
                  pl.BlockSpec((tb, qw), lambda i: (i, 1)),
                  pl.BlockSpec((HALO, qw), prev(1)),
                  pl.BlockSpec((HALO, qw), nxt(1)),
                  pl.BlockSpec((tb, AB_W), lambda i: (i, 0)),
                  pl.BlockSpec((3, qw), lambda i: (0, 0)),
                  pl.BlockSpec((3, sw), lambda i: (0, 0)),
                  pl.BlockSpec((1, AB_W), lambda i: (0, 0)),
                  pl.BlockSpec((1, AB_W), lambda i: (0, 0))],
        out_specs=[pl.BlockSpec((tb, qw), lambda i: (i, 0)),
                   pl.BlockSpec((nch, hw, GDN_CHUNK), lambda i: (i, 0, 0)),
                   pl.BlockSpec((tb, sw), lambda i: (i, 0)),
                   pl.BlockSpec((tb, AB_W), lambda i: (i, 0)),
                   pl.BlockSpec((nch, 2, AUX_ROWS, LANES), lambda i: (i, 0, 0, 0))],
        out_shape=[jax.ShapeDtypeStruct((t, qw), BF16),
                   jax.ShapeDtypeStruct((t // GDN_CHUNK, hw, GDN_CHUNK), BF16),
                   jax.ShapeDtypeStruct((t, sw), BF16),
                   jax.ShapeDtypeStruct((t, AB_W), F32),
                   jax.ShapeDtypeStruct((t // GDN_CHUNK, 2, AUX_ROWS, LANES), F32)],
        compiler_params=_cparams("parallel"),
        name="conv",
    )(zin, zin, zin, zin, zin, zin, ab, conv_qkv, conv_sc, arow, dtrow)


def _gdn_kernel(qf_ref, kf_ref, vf_ref, ktf_ref, gsf_ref, auxf_ref,
                qb_ref, kb_ref, vb_ref, ktb_ref, gsb_ref, auxb_ref, of_ref, ob_ref, s_scr):
    @pl.when(pl.program_id(1) == 0)
    def _():
        s_scr[...] = jnp.zeros_like(s_scr)

    c = GDN_CHUNK
    nh = GDN_HEADS
    ri = lax.broadcasted_iota(jnp.int32, (c, c), 0)
    ci = lax.broadcasted_iota(jnp.int32, (c, c), 1)
    incl_d = (ri >= ci, ri <= ci)
    strict_d = (ri > ci, ri < ci)
    refs = ((qf_ref, kf_ref, vf_ref, ktf_ref, gsf_ref, auxf_ref, of_ref),
            (qb_ref, kb_ref, vb_ref, ktb_ref, gsb_ref, auxb_ref, ob_ref))
    units = [(d, h) for d in range(2) for h in range(nh)]
    every = lambda f: [f(n, d, h) for n, (d, h) in enumerate(units)]
    hsl = lambda h: slice(h * GDN_DK, (h + 1) * GDN_DK)
    col = lambda d, j: slice(d * (AB_W // 2) + j, d * (AB_W // 2) + j + 1)

    gsv = [gsf_ref[...], gsb_ref[...]]
    q = every(lambda n, d, h: refs[d][0][:, hsl(h)])
    k = every(lambda n, d, h: refs[d][1][:, hsl(h)].astype(F32))
    v = every(lambda n, d, h: refs[d][2][:, hsl(h)].astype(F32))
    kt = every(lambda n, d, h: refs[d][3][0, hsl(h), :])
    gcol = every(lambda n, d, h: gsv[d][:, col(d, h)])
    beta = every(lambda n, d, h: gsv[d][:, col(d, nh + h)])
    egc = every(lambda n, d, h: gsv[d][:, col(d, 2 * nh + h)])
    grow = every(lambda n, d, h: refs[d][5][0, 0, h:h + 1, 0:c])
    kdrow = every(lambda n, d, h: refs[d][5][0, 0, nh + h:nh + h + 1, 0:c])
    etot = every(lambda n, d, h: refs[d][5][0, 0, 2 * nh + h:2 * nh + h + 1, :])

    decay = every(lambda n, d, h: jnp.exp(jnp.where(incl_d[d], gcol[n] - grow[n], NEG_BIG)))
    kb = every(lambda n, d, h: k[n] * beta[n])
    gram = every(lambda n, d, h: _dot(jnp.concatenate([kb[n].astype(BF16), q[n]], axis=0), kt[n]))
    low = every(lambda n, d, h: jnp.where(strict_d[d], gram[n][:c] * decay[n], 0.0))
    attn = every(lambda n, d, h: jnp.where(incl_d[d], gram[n][c:] * decay[n], 0.0).astype(BF16))

    x = every(lambda n, d, h: jnp.concatenate([v[n] * beta[n], kb[n] * egc[n]], axis=-1))
    lp = [l.astype(BF16) for l in low]
    x = every(lambda n, d, h: x[n] - _dot(lp[n], x[n].astype(BF16)))
    p = 2
    while p < c:
        lp = [_dot(l, l).astype(BF16) for l in lp]
        x = every(lambda n, d, h: x[n] + _dot(lp[n], x[n].astype(BF16)))
        p *= 2

    s = [s_scr[n] for n in range(len(units))]
    sb = [t.astype(BF16) for t in s]
    qg = every(lambda n, d, h: (q[n].astype(F32) * egc[n]).astype(BF16))
    ws = every(lambda n, d, h: _dot(jnp.concatenate([x[n][:, GDN_DV:].astype(BF16), qg[n]], axis=0), sb[n]))
    vnb = every(lambda n, d, h: (x[n][:, :GDN_DV] - ws[n][:c]).astype(BF16))
    o = every(lambda n, d, h: ws[n][c:] + _dot(attn[n], vnb[n]))
    for n, (d, h) in enumerate(units):
        refs[d][6][:, hsl(h)] = o[n].astype(BF16)
    ktd = every(lambda n, d, h: (kt[n].astype(F32) * kdrow[n]).astype(BF16))
    for n in range(len(units)):
        s_scr[n] = s[n] * etot[n] + _dot(ktd[n], vnb[n])


def _gdn(qkvn, kt, gs, aux, batch, seq):
    t = qkvn.shape[0]
    hw = GDN_HEADS * GDN_DK
    c = GDN_CHUNK
    nc = seq // c
    fwd = lambda b, j: b * nc + j
    bwd = lambda b, j: b * nc + nc - 1 - j

    def specs(ch, d):
        return [pl.BlockSpec((c, hw), lambda b, j: (ch(b, j), 0)),
                pl.BlockSpec((c, hw), lambda b, j: (ch(b, j), 1)),
                pl.BlockSpec((c, hw), lambda b, j: (ch(b, j), 2)),
                pl.BlockSpec((1, hw, c), lambda b, j: (ch(b, j), 0, 0)),
                pl.BlockSpec((c, AB_W), lambda b, j: (ch(b, j), 0)),
                pl.BlockSpec((1, 1, AUX_ROWS, LANES), lambda b, j: (ch(b, j), d, 0, 0))]

    return pl.pallas_call(
        _gdn_kernel,
        grid=(batch, nc),
        in_specs=specs(fwd, 0) + specs(bwd, 1),
        out_specs=[pl.BlockSpec((c, hw), lambda b, j: (fwd(b, j), 0)),
                   pl.BlockSpec((c, hw), lambda b, j: (bwd(b, j), 0))],
        out_shape=[jax.ShapeDtypeStruct((t, hw), BF16), jax.ShapeDtypeStruct((t, hw), BF16)],
        scratch_shapes=[pltpu.VMEM((2 * GDN_HEADS, GDN_DK, GDN_DV), F32)],
        compiler_params=_cparams("parallel", "arbitrary"),
        name="gdn",
    )(qkvn, qkvn, qkvn, kt, gs, aux, qkvn, qkvn, qkvn, kt, gs, aux)


def _mix_kernel(of_ref, ob_ref, z_ref, mg_ref, yb_ref, x_ref, mod_ref, onorm_ref, post1_ref, pre2_ref,
                wa_ref, wb_ref, wo_ref, x1_ref, h2t_ref, gated_scr):
    o = of_ref[...].astype(F32) + ob_ref[...].astype(F32)
    z = z_ref[...].astype(F32)
    for h in range(GDN_HEADS):
        hs = slice(h * GDN_DV, (h + 1) * GDN_DV)
        gated_scr[:, hs] = (_rms(o[:, hs], onorm_ref[...]) * _silu(z[:, hs])).astype(BF16)
    ya = _dot(gated_scr[...], wa_ref[...])
    yb = _dot(yb_ref[...], wb_ref[...])
    d = ya.shape[1]
    mg = mg_ref[...].astype(F32)
    mix = jax.nn.sigmoid(mg[:, :d]) * ya + jax.nn.sigmoid(mg[:, d:]) * yb
    mo = _dot(mix.astype(BF16), wo_ref[...])
    x1 = x_ref[...] + mod_ref[0, 2:3, :] * _rms(mo, post1_ref[...])
    x1_ref[...] = x1
    h2 = _rms(x1, pre2_ref[...]) * (1.0 + mod_ref[0, 4:5, :]) + mod_ref[0, 3:4, :]
    h2t = h2.T
    for nb in range(h2t_ref.shape[0]):
        h2t_ref[nb] = h2t[:, nb * TOKEN_BLOCK:(nb + 1) * TOKEN_BLOCK].astype(BF16)


def _mix(o_f, o_b, zin, yb, x2, mod, onorm, post1, pre2, wa, wb, wo, seq, tm):
    t, d = x2.shape
    hw = o_f.shape[1]
    tps = seq // tm
    zcol = zin.shape[1] // hw - 1
    mgcol = (zin.shape[1] - hw) // (2 * d) - 1
    full = lambda shape: pl.BlockSpec(shape, lambda i: (0,) * len(shape))
    return pl.pallas_call(
        _mix_kernel,
        grid=(t // tm,),
        in_specs=[pl.BlockSpec((tm, hw), lambda i: (i, 0)),
                  pl.BlockSpec((tm, hw), lambda i: (i, 0)),
                  pl.BlockSpec((tm, hw), lambda i: (i, zcol)),
                  pl.BlockSpec((tm, 2 * d), lambda i: (i, mgcol)),
                  pl.BlockSpec((tm, yb.shape[1]), lambda i: (i, 0)),
                  pl.BlockSpec((tm, d), lambda i: (i, 0)),
                  pl.BlockSpec((1, MOD_ROWS, d), lambda i: (i // tps, 0, 0)),
                  full((1, GDN_DV)), full((1, d)), full((1, d)),
                  full(wa.shape), full(wb.shape), full(wo.shape)],
        out_specs=[pl.BlockSpec((tm, d), lambda i: (i, 0)),
                   pl.BlockSpec((tm // TOKEN_BLOCK, d, TOKEN_BLOCK), lambda i: (i, 0, 0))],
        out_shape=[jax.ShapeDtypeStruct((t, d), F32),
                   jax.ShapeDtypeStruct((t // TOKEN_BLOCK, d, TOKEN_BLOCK), BF16)],
        scratch_shapes=[pltpu.VMEM((tm, hw), BF16)],
        compiler_params=_cparams("parallel"),
        name="mix",
    )(o_f, o_b, zin, zin, yb, x2, mod, onorm, post1, pre2, wa, wb, wo)


def _pair_bits(x):
    u = pltpu.bitcast(x.astype(BF16).astype(F32), GATE_BITS)
    return u | (u >> 16)


def _rows_from_pair_bits(row, reps):
    return pltpu.bitcast(jnp.broadcast_to(row, (reps, row.shape[1])), BF16)


def _merge_sort_pairs(lo, hi):
    def merge(lo, hi, r):
        step = 2 * r
        if step < hi - lo:
            yield from merge(lo, hi, step)
            yield from merge(lo + r, hi, step)
            yield from ((i, i + r) for i in range(lo + r, hi - r, step))
        else:
            yield (lo, lo + r)

    if hi > lo:
        mid = lo + (hi - lo) // 2
        yield from _merge_sort_pairs(lo, mid)
        yield from _merge_sort_pairs(mid + 1, hi)
        yield from merge(lo, hi, 1)


def _top_vals(s, k):
    sub = 8
    assert s.shape[0] == k * sub and k & (k - 1) == 0
    c = [s[i * sub:(i + 1) * sub, :] for i in range(k)]

    def exchange(i, j):
        c[i], c[j] = jnp.maximum(c[i], c[j]), jnp.minimum(c[i], c[j])

    for i, j in _merge_sort_pairs(0, k - 1):
        exchange(i, j)
    shift = sub // 2
    while shift:
        other = [pltpu.roll(x, shift, axis=0) for x in c]
        c = [jnp.maximum(c[i], other[k - 1 - i]) for i in range(k)]
        dist = k // 2
        while dist:
            for i in range(k):
                if i & dist == 0:
                    exchange(i, i + dist)
            dist //= 2
        shift //= 2
    return [x[0:1, :] for x in c]


def _peerq_kernel(h2t_ref, wq_ref, keys_ref, a_ref, b_ref, thr_ref):
    qt = _dot(wq_ref[...], h2t_ref[0]).astype(BF16)
    dq = keys_ref.shape[2]
    kk = PEER_TOPK
    a_sorted, b_sorted, b_full = [], [], []
    for h in range(PEER_HEADS):
        s0 = _dot(keys_ref[2 * h], qt[(2 * h) * dq:(2 * h + 1) * dq, :])
        s1 = _dot(keys_ref[2 * h + 1], qt[(2 * h + 1) * dq:(2 * h + 2) * dq, :])
        v0 = _top_vals(s0, kk)
        v1 = _top_vals(s1, kk)
        a_ref[h] = _pair_bits(jnp.exp(s0 - v0[0]))
        b_full.append(jnp.exp(s1 - v1[0]))
        a_sorted.append([jnp.exp(v - v0[0]) for v in v0])
        b_sorted.append([jnp.exp(v - v1[0]) for v in v1])
    a_k = [jnp.concatenate([a_sorted[h][r] for h in range(PEER_HEADS)], axis=0) for r in range(kk)]
    b_k = [jnp.concatenate([b_sorted[h][r] for h in range(PEER_HEADS)], axis=0) for r in range(kk)]
    pairs = [(r0, r1) for r0 in range(kk) for r1 in range(kk // (r0 + 1))]
    cand = [a_k[r0] * b_k[r1] for r0, r1 in pairs]
    cur = list(cand)
    z = jnp.zeros_like(cand[0])
    m = z
    for _ in range(kk):
        m = jnp.maximum(functools.reduce(jnp.maximum, cur), 0.0)
        z = z + m
        cur = [jnp.where(x == m, -1.0, x) for x in cur]
    inv_z = 1.0 / z
    a_b = [a.astype(BF16) for a in a_k]
    b_b = [(b * inv_z).astype(BF16) for b in b_k]
    thr = functools.reduce(jnp.minimum, [
        jnp.where(cand[n] >= m, (a_b[r0] * b_b[r1]).astype(F32), jnp.inf) for n, (r0, r1) in enumerate(pairs)])
    thr_ref[...] = _pair_bits(thr)
    for h in range(PEER_HEADS):
        b_ref[h, 0] = (b_full[h] * inv_z[h:h + 1, :]).astype(BF16)


def _peerq(h2t, wq_t, keys):
    nblk, d, tm = h2t.shape
    t = nblk * tm
    nk = keys.shape[1]
    full = lambda shape: pl.BlockSpec(shape, lambda i: (0,) * len(shape))
    return pl.pallas_call(
        _peerq_kernel,
        grid=(nblk,),
        in_specs=[pl.BlockSpec((1, d, tm), lambda i: (i, 0, 0)), full(wq_t.shape), full(keys.shape)],
        out_specs=[pl.BlockSpec((PEER_HEADS, nk, tm), lambda i: (0, 0, i)),
                   pl.BlockSpec((PEER_HEADS, 1, nk, tm), lambda i: (0, i, 0, 0)),
                   pl.BlockSpec((PEER_HEADS, tm), lambda i: (0, i))],
        out_shape=[jax.ShapeDtypeStruct((PEER_HEADS, nk, t), GATE_BITS),
                   jax.ShapeDtypeStruct((PEER_HEADS, nblk, nk, tm), BF16),
                   jax.ShapeDtypeStruct((PEER_HEADS, t), GATE_BITS)],
        compiler_params=_cparams("parallel"),
        name="peerq",
    )(h2t, wq_t, keys)


def _gelu(x):
    return 0.5 * x * (1.0 + lax.erf(x * (2.0 ** -0.5)))


def _peer_kernel(h2t_ref, u_ref, vt_ref, a_ref, b_ref, thr_ref, ft_ref, act0_ref, act1_ref, w20_ref, w21_ref,
                 *, ne, total):
    s = pl.program_id(0)
    nk = a_ref.shape[1]
    nblk, te, tn = act0_ref.shape
    rows = te // nk
    d = ft_ref.shape[1]
    pack = 2 * 8
    a_pieces, v_pieces = 4, 4

    @pl.when(s == 0)
    def _():
        for ref in (act0_ref, act1_ref, w20_ref, w21_ref):
            ref[...] = jnp.zeros_like(ref)

    iv = jnp.clip(s - 1, 0, total - 1) % ne

    @pl.when((jnp.maximum(s - 2, 0) % ne) == 0)
    def _():
        ft_ref[...] = jnp.zeros_like(ft_ref)

    def body(act_w, act_r, w2_w, w2_r):
        for nb in range(nblk):
            ns = slice(nb * tn, (nb + 1) * tn)
            for r in range(rows):
                if r % (rows // a_pieces) == 0:
                    pa = r // (rows // a_pieces)
                    ps = slice(pa * (te // a_pieces), (pa + 1) * (te // a_pieces))
                    act_w[nb, ps, :] = _dot(u_ref[ps, :], h2t_ref[nb])
                a16 = [_rows_from_pair_bits(a_ref[h, pl.ds(iv * rows + r, 1), ns], pack // 2)
                       for h in range(PEER_HEADS)]
                t16 = [_rows_from_pair_bits(thr_ref[h:h + 1, ns], pack // 2) for h in range(PEER_HEADS)]
                for g in range(nk // pack):
                    gs = slice(r * nk + g * pack, r * nk + (g + 1) * pack)
                    wsum = None
                    for h in range(PEER_HEADS):
                        p = a16[h] * b_ref[h, nb, g * pack:(g + 1) * pack, :]
                        term = jnp.where(p >= t16[h], p, jnp.zeros_like(p))
                        wsum = term if wsum is None else wsum + term
                    w2_w[nb, gs, :] = _gelu(act_r[nb, gs, :].astype(BF16)) * wsum
                if (r + 1) % (rows // v_pieces) == 0:
                    pv = r // (rows // v_pieces)
                    ds = slice(pv * (d // v_pieces), (pv + 1) * (d // v_pieces))
                    ft_ref[nb, ds, :] += _dot(vt_ref[ds, :], w2_r[nb])

    @pl.when(s % 2 == 0)
    def _():
        body(act0_ref, act1_ref, w20_ref, w21_ref)

    @pl.when(s % 2 == 1)
    def _():
        body(act1_ref, act0_ref, w21_ref, w20_ref)


def _peer(h2t, u, vt, a, b, thr, tm, te):
    nblk, d, tn = h2t.shape
    nk = a.shape[1]
    ne = u.shape[0] // te
    tb = tm // tn
    total = (nblk // tb) * ne
    assert te % (2 * nk) == 0
    item = lambda s, lag: jnp.clip(s - lag, 0, total - 1)
    return pl.pallas_call(
        functools.partial(_peer_kernel, ne=ne, total=total),
        grid=(total + 2,),
        in_specs=[pl.BlockSpec((tb, d, tn), lambda s: (item(s, 0) // ne, 0, 0)),
                  pl.BlockSpec((te, d), lambda s: (item(s, 0) % ne, 0)),
                  pl.BlockSpec((d, te), lambda s: (0, item(s, 2) % ne)),
                  pl.BlockSpec((PEER_HEADS, nk, tm), lambda s: (0, 0, item(s, 1) // ne)),
                  pl.BlockSpec((PEER_HEADS, tb, nk, tn), lambda s: (0, item(s, 1) // ne, 0, 0)),
                  pl.BlockSpec((PEER_HEADS, tm), lambda s: (0, item(s, 1) // ne))],
        out_specs=pl.BlockSpec((tb, d, tn), lambda s: (item(s, 2) // ne, 0, 0)),
        out_shape=jax.ShapeDtypeStruct((nblk, d, tn), F32),
        scratch_shapes=[pltpu.VMEM((tb, te, tn), F32), pltpu.VMEM((tb, te, tn), F32),
                        pltpu.VMEM((tb, te, tn), BF16), pltpu.VMEM((tb, te, tn), BF16)],
        compiler_params=_cparams("arbitrary"),
        name="peer",
    )(h2t, u, vt, a, b, thr)


def _final_kernel(ft_ref, x1_ref, mod_ref, post2_ref, out_ref):
    for nb in range(ft_ref.shape[0]):
        rs = slice(nb * TOKEN_BLOCK, (nb + 1) * TOKEN_BLOCK)
        out_ref[rs, :] = x1_ref[rs, :] + mod_ref[0, 5:6, :] * _rms(ft_ref[nb].T, post2_ref[...])


def _final(ft, x1, mod, post2, seq, tm):
    nblk, d, tn = ft.shape
    t = nblk * tn
    tps = seq // tm
    return pl.pallas_call(
        _final_kernel,
        grid=(t // tm,),
        in_specs=[pl.BlockSpec((tm // tn, d, tn), lambda i: (i, 0, 0)),
                  pl.BlockSpec((tm, d), lambda i: (i, 0)),
                  pl.BlockSpec((1, MOD_ROWS, d), lambda i: (i // tps, 0, 0)),
                  pl.BlockSpec((1, d), lambda i: (0, 0))],
        out_specs=pl.BlockSpec((tm, d), lambda i: (i, 0)),
        out_shape=jax.ShapeDtypeStruct((t, d), F32),
        compiler_params=_cparams("parallel"),
        name="final",
    )(ft, x1, mod, post2)


def _tile(n, pref):
    return pref if n % pref == 0 else n


def _layer(x, mod, p):
    bsz, seq, d = x.shape
    t = bsz * seq
    x2 = x.reshape(t, d)
    tm = _tile(seq, 512)
    zin, ab = _inproj(x2, mod, p["pre1"], p["w_main"], p["w_ab"], seq, tm)
    qkvn, kt, yb, gs, aux = _conv(zin, ab, p["conv_qkv"], p["conv_sc"], p["arow"], p["dtrow"], seq, _tile(seq, 256))
    o_f, o_b = _gdn(qkvn, kt, gs, aux, bsz, seq)
    x1, h2t = _mix(o_f, o_b, zin, yb, x2, mod, p["onorm"], p["post1"], p["pre2"], p["wa"], p["wb"], p["wo"], seq, tm)
    a, b, thr = _peerq(h2t, p["wq_t"], p["keys"])
    ft = _peer(h2t, p["u"], p["vt"], a, b, thr, _tile(seq, 1024), 512)
    out = _final(ft, x1, mod, p["post2"], seq, tm)
    return out.reshape(bsz, seq, d)


def _lane_rows(vals_f, vals_b):
    def half(v):
        return jnp.concatenate([jnp.tile(v, 5), jnp.zeros((AB_W // 2 - 5 * GDN_HEADS,), F32)])
    return jnp.concatenate([half(vals_f), half(vals_b)]).reshape(1, AB_W)


def _prep(l, pre_norm1, post_norm1, pre_norm2, post_norm2, w_in, conv_qkv, a_log, dt_bias, out_norm,
          w_branch_a, conv_sc, w_branch_b, w_out, w_peer_q, peer_sub_keys, peer_u, peer_v):
    d = w_in.shape[1]
    hw = GDN_HEADS * GDN_DK
    qkv_w = 2 * hw + GDN_HEADS * GDN_DV
    z_w = GDN_HEADS * GDN_DV
    nh2 = 2 * GDN_HEADS
    sc_w = 3 * conv_sc.shape[2]
    o_z, o_b, o_a, o_sc = qkv_w, qkv_w + z_w, qkv_w + z_w + nh2, qkv_w + z_w + 2 * nh2
    o_mg = o_sc + sc_w
    w = w_in[l]
    w_main = jnp.concatenate([w[:, :qkv_w], w[:, o_sc:o_mg], w[:, o_mg:], w[:, o_z:o_b]], axis=1).astype(BF16)
    beta = w[:, o_b:o_a]
    alpha = w[:, o_a:o_sc]
    pad = jnp.zeros((d, AB_W // 2 - 5 * GDN_HEADS), F32)

    def half(i):
        al = alpha[:, i * GDN_HEADS:(i + 1) * GDN_HEADS]
        return jnp.concatenate([al, beta[:, i * GDN_HEADS:(i + 1) * GDN_HEADS], al, al, al, pad], axis=1)

    w_ab = jnp.concatenate([half(0), half(1)], axis=1).astype(BF16)
    ea = jnp.exp(a_log[l].astype(F32))
    nkeys, dq = peer_sub_keys.shape[3], peer_sub_keys.shape[4]
    return dict(
        pre1=pre_norm1[l].reshape(1, d), post1=post_norm1[l].reshape(1, d),
        pre2=pre_norm2[l].reshape(1, d), post2=post_norm2[l].reshape(1, d),
        w_main=w_main, w_ab=w_ab, conv_qkv=conv_qkv[l], conv_sc=conv_sc[l],
        arow=_lane_rows(ea[0], ea[1]), dtrow=_lane_rows(dt_bias[l, 0], dt_bias[l, 1]),
        onorm=out_norm[l].reshape(1, GDN_DV),
        wa=w_branch_a[l].astype(BF16), wb=w_branch_b[l].astype(BF16), wo=w_out[l].astype(BF16),
        wq_t=w_peer_q[l].T.astype(BF16),
        keys=peer_sub_keys[l].reshape(PEER_HEADS * 2, nkeys, dq).astype(BF16),
        u=peer_u[l].astype(BF16), vt=peer_v[l].T.astype(BF16))


def kernel(x_prompt, x_sample, c_prompt, c_sample, pre_norm1, post_norm1, pre_norm2, post_norm2, w_ada, b_ada, w_in, conv_qkv, a_log, dt_bias, out_norm, w_branch_a, conv_sc, w_branch_b, w_out, w_peer_q, peer_sub_keys, peer_u, peer_v):
    depth = w_in.shape[0]
    d = x_prompt.shape[2]
    nb = x_prompt.shape[0]
    xs = [x_prompt, x_sample]
    c_all = jnp.concatenate([c_prompt, c_sample], axis=0)
    for l in range(depth):
        p = _prep(l, pre_norm1, post_norm1, pre_norm2, post_norm2, w_in, conv_qkv, a_log, dt_bias, out_norm,
                  w_branch_a, conv_sc, w_branch_b, w_out, w_peer_q, peer_sub_keys, peer_u, peer_v)
        mod = _ada(c_all, w_ada[l], b_ada[l]).reshape(c_all.shape[0], N_MOD, d)
        mod = jnp.pad(mod, ((0, 0), (0, MOD_ROWS - N_MOD), (0, 0)))
        xs = [_layer(xs[0], mod[:nb], p), _layer(xs[1], mod[nb:], p)]
    return (xs[0], xs[1])
```
